```python
import math
import jax, jax.numpy as jnp
from jax import lax
import numpy as np


D_MODEL = 2048
BATCH = 2
SEQ = 4096
DEPTH = 4
DEC_BATCH = 8
DEC_SEQ = 1
PAST_LEN = 16384
PAGE_SIZE = 128

N_EVEN = (DEPTH + 1) // 2
N_ODD = DEPTH // 2
HD_A = 128
H_A = D_MODEL // (2 * HD_A)
A_WIDTH = H_A * HD_A
DILATED_PATTERNS = ((128, 1), (512, 4), (2048, 16))
WINDOW_MAX = 2048
D_INNER = D_MODEL // 2
SSM_HEADDIM = 64
H_B = D_INNER // SSM_HEADDIM
SSM_GROUPS = 4
D_STATE = 128
CONV_W = 4
CONV_DIM = D_INNER + 2 * SSM_GROUPS * D_STATE
SSD_CHUNK = 128
EVEN_IN = 3 * A_WIDTH + D_INNER + CONV_DIM + H_B
EVEN_MIX = A_WIDTH + D_INNER
HD_C = 128
H_C = D_MODEL // (2 * HD_C)
C_WIDTH = H_C * 2 * HD_C
ODD_IN = 3 * C_WIDTH
Q_BLOCK = 128
N_KEYS = 128
N_EXPERTS = N_KEYS * N_KEYS
PEER_HEADS = 8
PEER_DKEY = 256
PEER_DHALF = PEER_DKEY // 2
PEER_TOPK_HALF = 16
PEER_TOPK = 16
PEER_BLOCK = 128
EPS = 1e-6

kernel_name = 'hybrid_dilated_ssd_diffattn_peer_step'


def rmsnorm(x, w):
    x32 = x.astype(jnp.float32)
    y = x32 * lax.rsqrt(jnp.mean(x32 * x32, axis=-1, keepdims=True) + EPS)
    return (y * w.astype(jnp.float32)).astype(x.dtype)


def adaln(c, w, b):
    return jnp.split(jax.nn.silu(c) @ w + b, 6, axis=-1)


def modulate(x, norm_w, shift, scale):
    return rmsnorm(x, norm_w) * (1.0 + scale[:, None, :]) + shift[:, None, :]


def dilated_branch_prompt(q, k, v, dil, steps):
    b, s, h, e = q.shape
    L = s // dil
    blk = steps
    nb = -(-L // blk)
    lp = nb * blk

    def to_sub(t, front):
        t = t.reshape(b, L, dil, h, e).transpose(0, 2, 1, 3, 4)
        return jnp.pad(t, ((0, 0), (0, 0), (front, lp - L), (0, 0), (0, 0)))

    qb = to_sub(q, 0).reshape(b, dil, nb, blk, h, e)
    kb = to_sub(k, blk).reshape(b, dil, nb + 1, blk, h, e)
    vb = to_sub(v, blk).reshape(b, dil, nb + 1, blk, h, e)
    kc = jnp.concatenate([kb[:, :, :-1], kb[:, :, 1:]], axis=3)
    vc = jnp.concatenate([vb[:, :, :-1], vb[:, :, 1:]], axis=3)
    i = jnp.arange(blk)[:, None]
    j = jnp.arange(2 * blk)[None, :]
    dist = blk + i - j
    key_pos = (jnp.arange(nb)[:, None, None] - 1) * blk + j[None]
    valid = (dist >= 0)[None] & (dist <= steps)[None] & (key_pos >= 0)
    sc = jnp.einsum('bdnihe,bdnjhe->bdnhij', qb, kc, preferred_element_type=jnp.float32) / math.sqrt(e)
    sc = jnp.where(valid[None, None, :, None], sc, -jnp.inf)
    lse = jax.nn.logsumexp(sc, axis=-1)
    p = jnp.exp(sc - lse[..., None])
    o = jnp.einsum('bdnhij,bdnjhe->bdnihe', p.astype(v.dtype), vc)
    o = o.reshape(b, dil, lp, h, e)[:, :, :L].transpose(0, 2, 1, 3, 4).reshape(b, s, h, e)
    lse = lse.transpose(0, 1, 2, 4, 3).reshape(b, dil, lp, h)[:, :, :L].transpose(0, 2, 1, 3).reshape(b, s, h)
    return o, lse


def dilated_branch_sample(q, ext_k, ext_v, wb, dil, steps):
    t = q.shape[1]
    e = q.shape[-1]
    idx = wb + jnp.arange(t)[:, None] - dil * jnp.arange(steps + 1)[None, :]
    valid = idx >= 0
    idx = jnp.maximum(idx, 0)
    kg = ext_k[:, idx]
    vg = ext_v[:, idx]
    sc = jnp.einsum('bthe,btshe->bths', q, kg, preferred_element_type=jnp.float32) / math.sqrt(e)
    sc = jnp.where(valid[None, :, None, :], sc, -jnp.inf)
    lse = jax.nn.logsumexp(sc, axis=-1)
    p = jnp.exp(sc - lse[..., None])
    o = jnp.einsum('bths,btshe->bthe', p.astype(vg.dtype), vg)
    return o, lse


def combine_dilated(branches, dtype):
    o = jnp.stack([br[0] for br in branches]).astype(jnp.float32)
    lse = jnp.stack([br[1] for br in branches])
    w = jax.nn.softmax(lse, axis=0)
    return jnp.sum(w[..., None] * o, axis=0).astype(dtype)


def ssd_chunked(x, dt, a, bm, cm, h0):
    b, t, h, p = x.shape
    g = bm.shape[2]
    q = min(SSD_CHUNK, t)
    nc = -(-t // q)
    pad = nc * q - t
    padt = lambda z: jnp.pad(z, [(0, 0), (0, pad)] + [(0, 0)] * (z.ndim - 2))
    xc = padt(x).astype(jnp.float32).reshape(b, nc, q, h, p)
    dtc = padt(dt).reshape(b, nc, q, h)
    bh = jnp.repeat(padt(bm), h // g, axis=2).astype(jnp.float32).reshape(b, nc, q, h, -1)
    ch = jnp.repeat(padt(cm), h // g, axis=2).astype(jnp.float32).reshape(b, nc, q, h, -1)
    cum = jnp.cumsum(dtc * a, axis=2)
    causal = jnp.tril(jnp.ones((q, q), dtype=bool))
    seg = jnp.where(causal[None, None, :, :, None], cum[:, :, :, None, :] - cum[:, :, None, :, :], -jnp.inf)
    cb = jnp.einsum('bcihn,bcjhn->bcijh', ch, bh) * jnp.exp(seg)
    y_intra = jnp.einsum('bcijh,bcjh,bcjhp->bcihp', cb, dtc, xc)
    decay_end = jnp.exp(cum[:, :, -1:, :] - cum) * dtc
    chunk_in = jnp.einsum('bcjh,bcjhn,bcjhp->bchpn', decay_end, bh, xc)
    chunk_decay = jnp.exp(cum[:, :, -1, :])

    def step(state, inp):
        dec, contrib = inp
        return dec[:, :, None, None] * state + contrib, state

    h_final, h_start = lax.scan(step, h0.astype(jnp.float32),
                                (jnp.moveaxis(chunk_decay, 1, 0), jnp.moveaxis(chunk_in, 1, 0)))
    h_start = jnp.moveaxis(h_start, 0, 1)
    y_inter = jnp.einsum('bcihn,bchpn->bcihp', ch, h_start) * jnp.exp(cum)[..., None]
    y = (y_intra + y_inter).reshape(b, nc * q, h, p)[:, :t]
    return y, h_final


def ssm_branch(z, xbc, conv_prefix, h0, conv_w, conv_b, dt_raw, dt_bias, a_log, d_skip, norm_w):
    b, t, _ = xbc.shape
    xpad = jnp.concatenate([conv_prefix.astype(xbc.dtype), xbc], axis=1)
    conv = conv_b + sum(xpad[:, k:k + t] * conv_w[k] for k in range(CONV_W))
    new_prefix = xpad[:, t:]
    act = jax.nn.silu(conv)
    xs, bm, cm = jnp.split(act, [D_INNER, D_INNER + SSM_GROUPS * D_STATE], axis=-1)
    xs = xs.reshape(b, t, H_B, SSM_HEADDIM)
    bm = bm.reshape(b, t, SSM_GROUPS, D_STATE)
    cm = cm.reshape(b, t, SSM_GROUPS, D_STATE)
    dt = jax.nn.softplus(dt_raw.astype(jnp.float32) + dt_bias.astype(jnp.float32))
    a = -jnp.exp(a_log.astype(jnp.float32))
    y, h_final = ssd_chunked(xs, dt, a, bm, cm, h0)
    y = y + d_skip.astype(jnp.float32)[:, None] * xs.astype(jnp.float32)
    y = y.reshape(b, t, D_INNER) * jax.nn.silu(z.astype(jnp.float32))
    y = y.reshape(b, t, SSM_GROUPS, D_INNER // SSM_GROUPS)
    y = y * lax.rsqrt(jnp.mean(y * y, axis=-1, keepdims=True) + EPS)
    y = y.reshape(b, t, D_INNER) * norm_w.astype(jnp.float32)
    return y.astype(xbc.dtype), new_prefix, h_final


def split_even(proj):
    cuts = [A_WIDTH, 2 * A_WIDTH, 3 * A_WIDTH, 3 * A_WIDTH + D_INNER, 3 * A_WIDTH + D_INNER + CONV_DIM]
    return jnp.split(proj, cuts, axis=-1)


def even_mixer_prompt(h, w_in, w_out, conv_w, conv_b, dt_bias, a_log, d_skip, norm_w):
    b, s, _ = h.shape
    q, k, v, z, xbc, dt_raw = split_even(h @ w_in)
    q = q.reshape(b, s, H_A, HD_A)
    k = k.reshape(b, s, H_A, HD_A)
    v = v.reshape(b, s, H_A, HD_A)
    ya = combine_dilated([dilated_branch_prompt(q, k, v, dil, win // dil) for win, dil in DILATED_PATTERNS], h.dtype)
    prefix0 = jnp.zeros((b, CONV_W - 1, CONV_DIM), h.dtype)
    h0 = jnp.zeros((b, H_B, SSM_HEADDIM, D_STATE), jnp.float32)
    yb, conv_state, ssm_state = ssm_branch(z, xbc, prefix0, h0, conv_w, conv_b, dt_raw, dt_bias, a_log, d_skip, norm_w)
    out = jnp.concatenate([ya.reshape(b, s, A_WIDTH), yb], axis=-1) @ w_out
    wb = min(WINDOW_MAX, s)
    return out, k[:, s - wb:], v[:, s - wb:], conv_state, ssm_state


def even_mixer_sample(h, buf_k, buf_v, conv_prev, ssm_prev, w_in, w_out, conv_w, conv_b, dt_bias, a_log, d_skip, norm_w):
    b, t, _ = h.shape
    q, k, v, z, xbc, dt_raw = split_even(h @ w_in)
    q = q.reshape(b, t, H_A, HD_A)
    k = k.reshape(b, t, H_A, HD_A)
    v = v.reshape(b, t, H_A, HD_A)
    wb = buf_k.shape[1]
    ext_k = jnp.concatenate([buf_k.astype(k.dtype), k], axis=1)
    ext_v = jnp.concatenate([buf_v.astype(v.dtype), v], axis=1)
    ya = combine_dilated([dilated_branch_sample(q, ext_k, ext_v, wb, dil, win // dil) for win, dil in DILATED_PATTERNS], h.dtype)
    yb, conv_state, ssm_state = ssm_branch(z, xbc, conv_prev, ssm_prev, conv_w, conv_b, dt_raw, dt_bias, a_log, d_skip, norm_w)
    out = jnp.concatenate([ya.reshape(b, t, A_WIDTH), yb], axis=-1) @ w_out
    return out, k, v, conv_state, ssm_state


def diff_lambda(lq1, lk1, lq2, lk2, lam_init):
    f = lambda t: t.astype(jnp.float32)
    return jnp.exp(jnp.sum(f(lq1) * f(lk1))) - jnp.exp(jnp.sum(f(lq2) * f(lk2))) + lam_init


def diff_scores(q, k):
    return jnp.einsum('bqhme,bkhme->bhmqk', q, k, preferred_element_type=jnp.float32) / math.sqrt(HD_C)


def diff_weights(sc, mask, lam):
    p = jax.nn.softmax(jnp.where(mask, sc, -jnp.inf), axis=-1)
    return p[:, :, 0] - lam * p[:, :, 1]


def diff_output(o, subln_w, lam_init, w_out):
    b, t = o.shape[:2]
    o = rmsnorm(o, subln_w) * (1.0 - lam_init)
    return o.reshape(b, t, C_WIDTH) @ w_out


def odd_mixer_prompt(h, w_in, w_out, lq1, lk1, lq2, lk2, subln_w, lam_init):
    b, s, _ = h.shape
    q, k, v = jnp.split(h @ w_in, 3, axis=-1)
    q = q.reshape(b, s, H_C, 2, HD_C)
    k = k.reshape(b, s, H_C, 2, HD_C)
    v = v.reshape(b, s, H_C, 2 * HD_C)
    lam = diff_lambda(lq1, lk1, lq2, lk2, lam_init)
    nb = s // Q_BLOCK
    key_pos = jnp.arange(s)

    def block(args):
        qb, bi = args
        qpos = bi * Q_BLOCK + jnp.arange(Q_BLOCK)
        a = diff_weights(diff_scores(qb, k), key_pos[None, :] <= qpos[:, None], lam)
        return jnp.einsum('bhqk,bkhe->bqhe', a.astype(v.dtype), v)

    qblocks = jnp.moveaxis(q.reshape(b, nb, Q_BLOCK, H_C, 2, HD_C), 1, 0)
    o = lax.map(block, (qblocks, jnp.arange(nb)))
    o = jnp.moveaxis(o, 0, 1).reshape(b, s, H_C, 2 * HD_C)
    return diff_output(o, subln_w, lam_init, w_out), k.reshape(b, s, H_C, 2 * HD_C), v


def odd_mixer_sample(h, cache_k, cache_v, layer, page_table, w_in, w_out, lq1, lk1, lq2, lk2, subln_w, lam_init):
    b, t, _ = h.shape
    q, k, v = jnp.split(h @ w_in, 3, axis=-1)
    q = q.reshape(b, t, H_C, 2, HD_C)
    k = k.reshape(b, t, H_C, 2, HD_C)
    v = v.reshape(b, t, H_C, 2 * HD_C)
    lam = diff_lambda(lq1, lk1, lq2, lk2, lam_init)
    past = page_table.shape[1] * cache_k.shape[2]
    kp = cache_k[layer, page_table].reshape(b, past, H_C, 2, HD_C)
    vp = cache_v[layer, page_table].reshape(b, past, H_C, 2 * HD_C)
    sc = jnp.concatenate([diff_scores(q, kp), diff_scores(q, k)], axis=-1)
    key_pos = jnp.arange(past + t)
    qpos = past + jnp.arange(t)
    a = diff_weights(sc, key_pos[None, :] <= qpos[:, None], lam)
    o = (jnp.einsum('bhqk,bkhe->bqhe', a[..., :past].astype(vp.dtype), vp)
         + jnp.einsum('bhqk,bkhe->bqhe', a[..., past:].astype(v.dtype), v))
    return diff_output(o, subln_w, lam_init, w_out), k.reshape(b, t, H_C, 2 * HD_C), v


def peer(h, wq, keys, u, v):
    t, d = h.shape
    q = (h @ wq).reshape(t, PEER_HEADS, 2, PEER_DHALF)
    s = jnp.einsum('thce,hcne->thcn', q, keys, preferred_element_type=jnp.float32)
    s1, i1 = lax.top_k(s[:, :, 0], PEER_TOPK_HALF)
    s2, i2 = lax.top_k(s[:, :, 1], PEER_TOPK_HALF)
    cand = (s1[..., :, None] + s2[..., None, :]).reshape(t, PEER_HEADS, -1)
    cidx = (i1[..., :, None] * N_KEYS + i2[..., None, :]).reshape(t, PEER_HEADS, -1)
    top, pos = lax.top_k(cand, PEER_TOPK)
    eidx = jnp.take_along_axis(cidx, pos, axis=-1)
    g = jax.nn.softmax(top, axis=-1)
    blk = min(PEER_BLOCK, t)
    nb = -(-t // blk)
    pad = nb * blk - t
    hp = jnp.pad(h, ((0, pad), (0, 0))).reshape(nb, blk, d)
    ep = jnp.pad(eidx, ((0, pad), (0, 0), (0, 0))).reshape(nb, blk, PEER_HEADS, PEER_TOPK)
    gp = jnp.pad(g, ((0, pad), (0, 0), (0, 0))).reshape(nb, blk, PEER_HEADS, PEER_TOPK)

    def expert_block(args):
        hb, eb, gb = args
        act = jax.nn.gelu(jnp.einsum('td,thkd->thk', hb, u[eb], preferred_element_type=jnp.float32), approximate=False)
        return jnp.einsum('thk,thkd->td', (gb * act).astype(h.dtype), v[eb])

    out = lax.map(expert_block, (hp, ep, gp))
    return out.reshape(nb * blk, d)[:t]


def setup_inputs(seed: int = 0) -> dict:
    key = jax.random.key(seed)
    ks = iter(jax.random.split(key, 48))
    nrm = lambda shape, scale: jax.random.normal(next(ks), shape, jnp.float32) * scale
    n_pages = PAST_LEN // PAGE_SIZE
    n_used = DEC_BATCH * n_pages
    n_pool = n_used + max(1, n_used // 4)
    wb = min(WINDOW_MAX, PAST_LEN)
    page_table = jax.random.permutation(next(ks), n_pool)[:n_used].reshape(DEC_BATCH, n_pages).astype(jnp.int32)
    dt0 = jnp.exp(jax.random.uniform(next(ks), (N_EVEN, H_B), jnp.float32, math.log(1e-3), math.log(1e-1)))
    return {
        'x_prompt': nrm((BATCH, SEQ, D_MODEL), 1.0),
        'x_sample': nrm((DEC_BATCH, DEC_SEQ, D_MODEL), 1.0),
        'c_prompt': nrm((BATCH, D_MODEL), 1.0),
        'c_sample': nrm((DEC_BATCH, D_MODEL), 1.0),
        'cache_win_k': nrm((N_EVEN, DEC_BATCH, wb, H_A, HD_A), 1.0),
        'cache_win_v': nrm((N_EVEN, DEC_BATCH, wb, H_A, HD_A), 1.0),
        'state_conv': nrm((N_EVEN, DEC_BATCH, CONV_W - 1, CONV_DIM), 1.0),
        'state_ssm': nrm((N_EVEN, DEC_BATCH, H_B, SSM_HEADDIM, D_STATE), 0.1),
        'cache_diff_k': nrm((N_ODD, n_pool, PAGE_SIZE, H_C, 2 * HD_C), 1.0),
        'cache_diff_v': nrm((N_ODD, n_pool, PAGE_SIZE, H_C, 2 * HD_C), 1.0),
        'page_table': page_table,
        'norm1_w': 1.0 + nrm((DEPTH, D_MODEL), 0.02),
        'norm2_w': 1.0 + nrm((DEPTH, D_MODEL), 0.02),
        'w_ada': nrm((DEPTH, D_MODEL, 6 * D_MODEL), 0.5 * D_MODEL ** -0.5),
        'b_ada': nrm((DEPTH, 6 * D_MODEL), 0.02),
        'w_in_even': nrm((N_EVEN, D_MODEL, EVEN_IN), D_MODEL ** -0.5),
        'w_out_even': nrm((N_EVEN, EVEN_MIX, D_MODEL), EVEN_MIX ** -0.5),
        'conv_w': nrm((N_EVEN, CONV_W, CONV_DIM), CONV_W ** -0.5),
        'conv_b': nrm((N_EVEN, CONV_DIM), 0.02),
        'dt_bias': dt0 + jnp.log(-jnp.expm1(-dt0)),
        'a_log': jnp.log(jax.random.uniform(next(ks), (N_EVEN, H_B), jnp.float32, 1.0, 16.0)),
        'd_skip': 1.0 + nrm((N_EVEN, H_B), 0.02),
        'ssm_norm_w': 1.0 + nrm((N_EVEN, D_INNER), 0.02),
        'w_in_odd': nrm((N_ODD, D_MODEL, ODD_IN), D_MODEL ** -0.5),
        'w_out_odd': nrm((N_ODD, C_WIDTH, D_MODEL), C_WIDTH ** -0.5),
        'lambda_q1': nrm((N_ODD, HD_C), 0.1),
        'lambda_k1': nrm((N_ODD, HD_C), 0.1),
        'lambda_q2': nrm((N_ODD, HD_C), 0.1),
        'lambda_k2': nrm((N_ODD, HD_C), 0.1),
        'subln_w': 1.0 + nrm((N_ODD, 2 * HD_C), 0.02),
        'peer_wq': nrm((DEPTH, D_MODEL, PEER_HEADS * PEER_DKEY), D_MODEL ** -0.5),
        'peer_keys': nrm((DEPTH, PEER_HEADS, 2, N_KEYS, PEER_DHALF), PEER_DHALF ** -0.5),
        'peer_u': nrm((DEPTH, N_EXPERTS, D_MODEL), D_MODEL ** -0.5),
        'peer_v': nrm((DEPTH, N_EXPERTS, D_MODEL), 0.5),
        'final_norm_w': 1.0 + nrm((D_MODEL,), 0.02),
    }


def reference(x_prompt, x_sample, c_prompt, c_sample, cache_win_k, cache_win_v, state_conv, state_ssm,
              cache_diff_k, cache_diff_v, page_table, norm1_w, norm2_w, w_ada, b_ada, w_in_even, w_out_even,
              conv_w, conv_b, dt_bias, a_log, d_skip, ssm_norm_w, w_in_odd, w_out_odd, lambda_q1, lambda_k1,
              lambda_q2, lambda_k2, subln_w, peer_wq, peer_keys, peer_u, peer_v, final_norm_w):
    xp, xs = x_prompt, x_sample
    pwk, pwv, pcv, pss, pdk, pdv = [], [], [], [], [], []
    swk, swv, scv, sss, sdk, sdv = [], [], [], [], [], []
    for l in range(DEPTH):
        sh1p, sc1p, g1p, sh2p, sc2p, g2p = adaln(c_prompt, w_ada[l], b_ada[l])
        sh1s, sc1s, g1s, sh2s, sc2s, g2s = adaln(c_sample, w_ada[l], b_ada[l])
        hp = modulate(xp, norm1_w[l], sh1p, sc1p)
        hs = modulate(xs, norm1_w[l], sh1s, sc1s)
        if l % 2 == 0:
            e = l // 2
            ew = (w_in_even[e], w_out_even[e], conv_w[e], conv_b[e], dt_bias[e], a_log[e], d_skip[e], ssm_norm_w[e])
            op, k_p, v_p, cv_p, ss_p = even_mixer_prompt(hp, *ew)
            os_, k_s, v_s, cv_s, ss_s = even_mixer_sample(hs, cache_win_k[e], cache_win_v[e], state_conv[e], state_ssm[e], *ew)
            pwk.append(k_p); pwv.append(v_p); pcv.append(cv_p); pss.append(ss_p)
            swk.append(k_s); swv.append(v_s); scv.append(cv_s); sss.append(ss_s)
        else:
            o_ = l // 2
            lam_init = 0.8 - 0.6 * math.exp(-0.3 * l)
            ow = (w_in_odd[o_], w_out_odd[o_], lambda_q1[o_], lambda_k1[o_], lambda_q2[o_], lambda_k2[o_], subln_w[o_], lam_init)
            op, k_p, v_p = odd_mixer_prompt(hp, *ow)
            os_, k_s, v_s = odd_mixer_sample(hs, cache_diff_k, cache_diff_v, o_, page_table, *ow)
            pdk.append(k_p); pdv.append(v_p)
            sdk.append(k_s); sdv.append(v_s)
        xp = xp + g1p[:, None, :] * op
        xs = xs + g1s[:, None, :] * os_
        hp = modulate(xp, norm2_w[l], sh2p, sc2p)
        hs = modulate(xs, norm2_w[l], sh2s, sc2s)
        pw = (peer_wq[l], peer_keys[l], peer_u[l], peer_v[l])
        xp = xp + g2p[:, None, :] * peer(hp.reshape(-1, D_MODEL), *pw).reshape(xp.shape)
        xs = xs + g2s[:, None, :] * peer(hs.reshape(-1, D_MODEL), *pw).reshape(xs.shape)
    y_prompt = rmsnorm(xp, final_norm_w)
    y_sample = rmsnorm(xs, final_norm_w)
    p_win_k = jnp.stack(pwk)
    p_win_v = jnp.stack(pwv)
    p_conv = jnp.stack(pcv)
    p_ssm = jnp.stack(pss)
    p_diff_k = jnp.stack(pdk)
    p_diff_v = jnp.stack(pdv)
    s_win_k = jnp.stack(swk)
    s_win_v = jnp.stack(swv)
    s_conv = jnp.stack(scv)
    s_ssm = jnp.stack(sss)
    s_diff_k = jnp.stack(sdk)
    s_diff_v = jnp.stack(sdv)
    return (y_prompt, y_sample, p_win_k, p_win_v, p_conv, p_ssm, p_diff_k, p_diff_v,
            s_win_k, s_win_v, s_conv, s_ssm, s_diff_k, s_diff_v)
```

```python
import functools
import math

import jax
import jax.numpy as jnp
from jax import lax
from jax.experimental import pallas as pl
from jax.experimental.pallas import tpu as pltpu

F32 = jnp.float32
BF16 = jnp.bfloat16
EPS = 1e-6
NEG_BIG = -1e30

HD_A = 128
HD_C = 128
DILATED_PATTERNS = ((128, 1), (512, 4), (2048, 16))
SSM_HEADDIM = 64
SSM_GROUPS = 4
D_STATE = 128
SSD_CHUNK = 128
PEER_HEADS = 8
PEER_TOPK = 16

V7X_VMEM_LIMIT = 58 * 1024 * 1024
LANE = 128
SUBLANE = 8

NT_DIMS = (((1,), (1,)), ((), ()))


def _cp(sem, vmem=V7X_VMEM_LIMIT):
    return pltpu.CompilerParams(dimension_semantics=sem, vmem_limit_bytes=vmem)


def _sigmoid(x):
    return 1.0 / (1.0 + jnp.exp(-x))


def _silu(x):
    return x * _sigmoid(x)


def _softplus(x):
    return jnp.maximum(x, 0.0) + jnp.log1p(jnp.exp(-jnp.abs(x)))


def _norm_modulate(x, nw, sh, sc):
    ms = jnp.mean(x * x, axis=-1, keepdims=True)
    y = x * lax.rsqrt(ms + EPS) * nw
    return y * (1.0 + sc) + sh


def _adaln_kernel(c_ref, w_ref, b_ref, o_ref):
    a_hi, a_lo = _split_bf16(_silu(c_ref[...]))
    w_hi, w_lo = _split_bf16(w_ref[0])
    o_ref[0] = _dot3(a_hi, a_lo, w_hi, w_lo, NN_DIMS) + b_ref[0]


def adaln_all(c_all, w_ada, b_ada):
    nl, d, n = w_ada.shape
    r = c_all.shape[0]
    tn = 1024
    return pl.pallas_call(
        _adaln_kernel,
        grid=(nl, n // tn),
        in_specs=[pl.BlockSpec((r, d), lambda l, j: (0, 0)),
                  pl.BlockSpec((1, d, tn), lambda l, j: (l, 0, j)),
                  pl.BlockSpec((1, 1, tn), lambda l, j: (l, 0, j))],
        out_specs=pl.BlockSpec((1, r, tn), lambda l, j: (l, 0, j)),
        out_shape=jax.ShapeDtypeStruct((nl, r, n), F32),
        compiler_params=_cp(("parallel", "parallel")),
        name="adaln",
    )(c_all, w_ada, b_ada.reshape(nl, 1, n))


def _normmod_mm_kernel(x_ref, nw_ref, sh_ref, sc_ref, w_ref, o_ref, ob_ref, h_ref):
    @pl.when(pl.program_id(1) == 0)
    def _():
        h_ref[...] = _norm_modulate(x_ref[...], nw_ref[...], sh_ref[0], sc_ref[0]).astype(BF16)

    o = jnp.dot(h_ref[...], w_ref[...], preferred_element_type=F32)
    o_ref[...] = o
    ob_ref[...] = o.astype(BF16)


def normmod_matmul(x, nw, sh, sc, w, *, tm, tn, rows_per_group):
    t, d = x.shape
    n = w.shape[1]
    r = sh.shape[1]
    tpg = rows_per_group // tm
    mod_spec = pl.BlockSpec((1, r, d), lambda i, j: (i // tpg, 0, 0))
    return pl.pallas_call(
        _normmod_mm_kernel,
        grid=(t // tm, n // tn),
        in_specs=[pl.BlockSpec((tm, d), lambda i, j: (i, 0)),
                  pl.BlockSpec((1, d), lambda i, j: (0, 0)),
                  mod_spec, mod_spec,
                  pl.BlockSpec((d, tn), lambda i, j: (0, j))],
        out_specs=[pl.BlockSpec((tm, tn), lambda i, j: (i, j)),
                   pl.BlockSpec((tm, tn), lambda i, j: (i, j))],
        out_shape=[jax.ShapeDtypeStruct((t, n), F32), jax.ShapeDtypeStruct((t, n), BF16)],
        scratch_shapes=[pltpu.VMEM((tm, d), BF16)],
        compiler_params=_cp(("parallel", "arbitrary")),
        name="normmod_matmul",
    )(x, nw, sh, sc, w)


def _mm_res_kernel(*refs, n_pairs):
    a_refs = refs[:n_pairs]
    w_refs = refs[n_pairs:2 * n_pairs]
    x_ref, g_ref, o_ref = refs[2 * n_pairs:]
    acc = jnp.dot(a_refs[0][...], w_refs[0][...], preferred_element_type=F32)
    for a_ref, w_ref in zip(a_refs[1:], w_refs[1:]):
        acc = acc + jnp.dot(a_ref[...], w_ref[...], preferred_element_type=F32)
    o_ref[...] = x_ref[...] + g_ref[0] * acc


def matmul_residual(a_list, w_list, x, gate, *, tm, tn, rows_per_group):
    t, n = x.shape
    r = gate.shape[1]
    tpg = rows_per_group // tm
    npairs = len(a_list)
    in_specs = ([pl.BlockSpec((tm, a.shape[1]), lambda i, j: (i, 0)) for a in a_list]
                + [pl.BlockSpec((w.shape[0], tn), lambda i, j: (0, j)) for w in w_list]
                + [pl.BlockSpec((tm, tn), lambda i, j: (i, j)),
                   pl.BlockSpec((1, r, tn), lambda i, j: (i // tpg, 0, j))])
    return pl.pallas_call(
        functools.partial(_mm_res_kernel, n_pairs=npairs),
        grid=(t // tm, n // tn),
        in_specs=in_specs,
        out_specs=pl.BlockSpec((tm, tn), lambda i, j: (i, j)),
        out_shape=jax.ShapeDtypeStruct((t, n), F32),
        compiler_params=_cp(("parallel", "parallel")),
        name="matmul_residual",
    )(*a_list, *w_list, x, gate)


def _rms_kernel(x_ref, w_ref, o_ref):
    x = x_ref[...]
    ms = jnp.mean(x * x, axis=-1, keepdims=True)
    o_ref[...] = x * lax.rsqrt(ms + EPS) * w_ref[...]


def rmsnorm_rows(x, w, *, tm):
    t, d = x.shape
    return pl.pallas_call(
        _rms_kernel,
        grid=(t // tm,),
        in_specs=[pl.BlockSpec((tm, d), lambda i: (i, 0)), pl.BlockSpec((1, d), lambda i: (0, 0))],
        out_specs=pl.BlockSpec((tm, d), lambda i: (i, 0)),
        out_shape=jax.ShapeDtypeStruct((t, d), F32),
        compiler_params=_cp(("parallel",)),
        name="final_rmsnorm",
    )(x, w)


def _top_values(s, k):
    vals = []
    for it in range(k):
        m = jnp.max(s, axis=0, keepdims=True)
        vals.append(m)
        if it + 1 < k:
            s = jnp.where(s == m, -jnp.inf, s)
    return vals


def _split_bf16(a):
    hi = a.astype(BF16)
    lo = (a - hi.astype(F32)).astype(BF16)
    return hi, lo


def _dot3(a_hi, a_lo, b_hi, b_lo, dims):
    f = functools.partial(lax.dot_general, dimension_numbers=dims, preferred_element_type=F32)
    return f(a_hi, b_hi) + (f(a_hi, b_lo) + f(a_lo, b_hi))


NN_DIMS = (((1,), (0,)), ((), ()))


def _peer_route_kernel(x_ref, nw_ref, sh_ref, sc_ref, wqh_ref, wql_ref, kh_ref, kl_ref,
                       h_ref, tau_ref, s1_ref, r_ref, s2_ref, e2_ref, qt_ref, *, heads, topk):
    h = _norm_modulate(x_ref[...], nw_ref[...], sh_ref[0], sc_ref[0])
    hb, hl = _split_bf16(h)
    h_ref[...] = hb
    qt_ref[...] = _dot3(wqh_ref[...], wql_ref[...], hb, hl, NT_DIMS)
    nk = kh_ref.shape[1]

    def head_body(hd, carry):
        s = []
        for c in range(2):
            idx = hd * 2 + c
            qh, ql = _split_bf16(qt_ref[pl.ds(pl.multiple_of(idx * nk, nk), nk), :])
            s.append(_dot3(kh_ref[idx], kl_ref[idx], qh, ql, NN_DIMS))
        v1 = _top_values(s[0], topk)
        v2 = _top_values(s[1], topk)
        v2s = jnp.concatenate(v2, axis=0)
        cand = jnp.concatenate([v1[a] + v2s for a in range(topk)], axis=0)
        mx = v1[0] + v2[0]
        z = jnp.zeros_like(mx)
        tau = mx
        for it in range(topk):
            tau = jnp.max(cand, axis=0, keepdims=True)
            z = z + jnp.exp(tau - mx)
            if it + 1 < topk:
                cand = jnp.where(cand == tau, -jnp.inf, cand)
        s1_ref[hd] = s[0]
        tau_ref[pl.ds(hd, 1), :] = tau
        r_ref[hd] = jnp.exp(s[0] - v1[0]) / z
        s2_ref[hd] = s[1]
        e2_ref[hd] = jnp.exp(s[1] - v2[0])
        return carry

    lax.fori_loop(0, heads, head_body, 0)


def peer_route(x, nw, sh, sc, wqt, keys2, *, tm, rows_per_group):
    t, d = x.shape
    r = sh.shape[1]
    nq = wqt[0].shape[0]
    nhc, nk, dh = keys2[0].shape
    heads = nhc // 2
    tpg = rows_per_group // tm
    mod_spec = pl.BlockSpec((1, r, d), lambda i: (i // tpg, 0, 0))
    fac_spec = pl.BlockSpec((heads, nk, tm), lambda i: (0, 0, i))
    fac_shape = jax.ShapeDtypeStruct((heads, nk, t), F32)
    wq_spec = pl.BlockSpec((nq, d), lambda i: (0, 0), pipeline_mode=pl.Buffered(1))
    key_spec = pl.BlockSpec((nhc, nk, dh), lambda i: (0, 0, 0), pipeline_mode=pl.Buffered(1))
    return pl.pallas_call(
        functools.partial(_peer_route_kernel, heads=heads, topk=PEER_TOPK),
        grid=(t // tm,),
        in_specs=[pl.BlockSpec((tm, d), lambda i: (i, 0)),
                  pl.BlockSpec((1, d), lambda i: (0, 0)),
                  mod_spec, mod_spec, wq_spec, wq_spec, key_spec, key_spec],
        out_specs=[pl.BlockSpec((tm, d), lambda i: (i, 0)), pl.BlockSpec((heads, tm), lambda i: (0, i)),
                   fac_spec, fac_spec, fac_spec, fac_spec],
        out_shape=[jax.ShapeDtypeStruct((t, d), BF16), jax.ShapeDtypeStruct((heads, t), F32),
                   fac_shape, fac_shape, fac_shape, fac_shape],
        scratch_shapes=[pltpu.VMEM((nq, tm), F32)],
        compiler_params=_cp(("parallel",)),
        name="peer_route",
    )(x, nw, sh, sc, wqt[0], wqt[1], keys2[0], keys2[1])


def _gelu_exact(a):
    return 0.5 * a * (1.0 + lax.erf(a * (1.0 / math.sqrt(2.0))))


def _peer_expert_kernel(h_ref, tau_ref, s1_ref, r_ref, s2_ref, e2_ref, u_ref, vt_ref, x_ref, g_ref,
                        o_ref, acc_ref, *, heads, ni):
    j = pl.program_id(1)

    @pl.when(j == 0)
    def _():
        acc_ref[...] = jnp.zeros_like(acc_ref)

    act = lax.dot_general(u_ref[...], h_ref[...], NT_DIMS, preferred_element_type=F32)
    act = _gelu_exact(act)
    gs = []
    for ii in range(ni):
        g = None
        for hd in range(heads):
            s1 = s1_ref[hd, ii:ii + 1, :]
            rr = r_ref[hd, ii:ii + 1, :]
            contrib = jnp.where(s2_ref[hd] + s1 >= tau_ref[hd:hd + 1, :], e2_ref[hd] * rr, 0.0)
            g = contrib if g is None else g + contrib
        gs.append(g)
    gate = jnp.concatenate(gs, axis=0)
    wa = (gate * act).astype(BF16)
    acc_ref[...] += jnp.dot(vt_ref[...], wa, preferred_element_type=F32)

    @pl.when(j == pl.num_programs(1) - 1)
    def _():
        o_ref[...] = x_ref[...] + g_ref[0] * acc_ref[...].T


def peer_experts(h, tau, s1, rfac, s2, e2, u, vt, x, gate, *, tm, ni, rows_per_group):
    t, d = x.shape
    heads, nk, _ = s1.shape
    r = gate.shape[1]
    tpg = rows_per_group // tm
    te = ni * nk
    row_spec = pl.BlockSpec((heads, ni, tm), lambda i, j: (0, j, i))
    full_spec = pl.BlockSpec((heads, nk, tm), lambda i, j: (0, 0, i))
    return pl.pallas_call(
        functools.partial(_peer_expert_kernel, heads=heads, ni=ni),
        grid=(t // tm, nk // ni),
        in_specs=[pl.BlockSpec((tm, d), lambda i, j: (i, 0)),
                  pl.BlockSpec((heads, tm), lambda i, j: (0, i)),
                  row_spec, row_spec, full_spec, full_spec,
                  pl.BlockSpec((te, d), lambda i, j: (j, 0)),
                  pl.BlockSpec((d, te), lambda i, j: (0, j)),
                  pl.BlockSpec((tm, d), lambda i, j: (i, 0), pipeline_mode=pl.Buffered(1)),
                  pl.BlockSpec((1, r, d), lambda i, j: (i // tpg, 0, 0))],
        out_specs=pl.BlockSpec((tm, d), lambda i, j: (i, 0)),
        out_shape=jax.ShapeDtypeStruct((t, d), F32),
        scratch_shapes=[pltpu.VMEM((d, tm), F32)],
        compiler_params=_cp(("parallel", "arbitrary")),
        name="peer_experts",
    )(h, tau, s1, rfac, s2, e2, u, vt, x, gate)


def _dilated_mult_table(tq, tk, n_off):
    o = jnp.arange(n_off, dtype=jnp.int32)[:, None, None]
    r = jnp.arange(tq, dtype=jnp.int32)[None, :, None]
    c = jnp.arange(tk, dtype=jnp.int32)[None, None, :]
    d = o * tk + r - c
    mult = jnp.zeros((n_off, tq, tk), F32)
    for win, dil in DILATED_PATTERNS:
        steps = win // dil
        ok = (d >= 0) & (d % dil == 0) & (d <= steps * dil)
        mult = mult + ok.astype(F32)
    return mult


def _dil_attn_kernel(q_ref, k_ref, v_ref, mult_ref, o_ref, m_ref, l_ref, acc_ref, *, t, n_off, scale):
    qi = pl.program_id(2)
    q = (q_ref[...].astype(F32) * scale).astype(BF16)
    m_ref[...] = jnp.full_like(m_ref, NEG_BIG)
    l_ref[...] = jnp.zeros_like(l_ref)
    acc_ref[...] = jnp.zeros_like(acc_ref)

    def body(kb, carry):
        off = qi - kb
        start = pl.multiple_of(kb * t, t)
        k = k_ref[pl.ds(start, t), :]
        v = v_ref[pl.ds(start, t), :]
        s = lax.dot_general(q, k, NT_DIMS, preferred_element_type=F32)
        mult = mult_ref[off]
        s = jnp.where(mult > 0.0, s, NEG_BIG)
        m_old = m_ref[...]
        m_new = jnp.maximum(m_old, jnp.max(s, axis=-1, keepdims=True))
        alpha = jnp.exp(m_old - m_new)
        p = jnp.exp(s - m_new) * mult
        l_ref[...] = alpha * l_ref[...] + jnp.sum(p, axis=-1, keepdims=True)
        acc_ref[...] = alpha * acc_ref[...] + jnp.dot(p.astype(BF16), v, preferred_element_type=F32)
        m_ref[...] = m_new
        return carry

    lax.fori_loop(jnp.maximum(qi - (n_off - 1), 0), qi + 1, body, 0)
    o_ref[...] = (acc_ref[...] / l_ref[...]).astype(o_ref.dtype)


def dilated_attention_prompt(proj_bf, *, batch, seq, heads, t):
    win_max = min(max(w for w, _ in DILATED_PATTERNS), seq)
    n_off = min(-(-win_max // t) + 1, seq // t)
    mult = _dilated_mult_table(t, t, n_off)
    nq = seq // t
    kern = functools.partial(_dil_attn_kernel, t=t, n_off=n_off, scale=1.0 / math.sqrt(HD_A))
    return pl.pallas_call(
        kern,
        grid=(batch, heads, nq),
        in_specs=[pl.BlockSpec((t, HD_A), lambda b, h, i: (b * nq + i, h)),
                  pl.BlockSpec((seq, HD_A), lambda b, h, i: (b, heads + h)),
                  pl.BlockSpec((seq, HD_A), lambda b, h, i: (b, 2 * heads + h)),
                  pl.BlockSpec((n_off, t, t), lambda b, h, i: (0, 0, 0))],
        out_specs=pl.BlockSpec((t, HD_A), lambda b, h, i: (b * nq + i, h)),
        out_shape=jax.ShapeDtypeStruct((batch * seq, heads * HD_A), BF16),
        scratch_shapes=[pltpu.VMEM((t, 1), F32), pltpu.VMEM((t, 1), F32), pltpu.VMEM((t, HD_A), F32)],
        compiler_params=_cp(("parallel", "parallel", "arbitrary")),
        name="dilated_attention",
    )(proj_bf, proj_bf, proj_bf, mult)


def _pair_cols(q, p, lane_lo):
    a = q[:, 2 * p:2 * p + 1]
    b = q[:, 2 * p + 1:2 * p + 2]
    return jnp.where(lane_lo, a, b)


def _ssd_kernel(z_ref, xbc_ref, dt_ref, cw_ref, cb_ref, dtb_ref, alog_ref, dsk_ref, nw_ref,
                y_ref, st_ref, xprev_ref, s_ref, *, d_inner, n_pairs, conv_w):
    c = pl.program_id(1)
    q = xbc_ref.shape[0]
    pre = SUBLANE

    @pl.when(c == 0)
    def _():
        xprev_ref[0:pre, :] = jnp.zeros((pre, xprev_ref.shape[1]), F32)
        s_ref[...] = jnp.zeros_like(s_ref)

    x_in = xbc_ref[...]
    xprev_ref[pre:pre + q, :] = x_in
    conv = cb_ref[...]
    for kk in range(conv_w):
        conv = conv + xprev_ref[pl.ds(pre - (conv_w - 1) + kk, q), :] * cw_ref[kk:kk + 1, :]
    xprev_ref[0:pre, :] = x_in[q - pre:q, :]
    act = _silu(conv)
    gn = SSM_GROUPS * D_STATE
    xs = act[:, :d_inner]
    bm = act[:, d_inner:d_inner + gn]
    cm = act[:, d_inner + gn:d_inner + 2 * gn]

    dt = _softplus(dt_ref[...] + dtb_ref[...])
    a = -jnp.exp(alog_ref[...])
    dta = dt * a
    row = lax.broadcasted_iota(jnp.int32, (q, q), 0)
    col = lax.broadcasted_iota(jnp.int32, (q, q), 1)
    causal = row >= col
    cum = jnp.dot(causal.astype(F32), dta, preferred_element_type=F32, precision=lax.Precision.HIGHEST)
    cum_t = cum.T
    dt_t = dt.T
    cum_end = cum[q - 1:q, :]
    wdec = jnp.exp(cum_end - cum) * dt
    ecum = jnp.exp(cum)
    cdec = jnp.exp(cum_end)
    lane_lo = lax.broadcasted_iota(jnp.int32, (q, LANE), 1) < SSM_HEADDIM
    lane_lo1 = lane_lo[0:1, :]
    pairs_per_group = (n_pairs * 2 // SSM_GROUPS) // 2

    ys = []
    for g in range(SSM_GROUPS):
        bg = bm[:, g * D_STATE:(g + 1) * D_STATE]
        cg = cm[:, g * D_STATE:(g + 1) * D_STATE].astype(BF16)
        cb = lax.dot_general(cg, bg.astype(BF16), NT_DIMS, preferred_element_type=F32)
        bg_t = bg.T.astype(BF16)
        for pp in range(pairs_per_group):
            p = g * pairs_per_group + pp
            xp = xs[:, p * LANE:(p + 1) * LANE]
            xpb = xp.astype(BF16)
            y_intra = None
            for hh in range(2):
                h = 2 * p + hh
                seg = cum[:, h:h + 1] - cum_t[h:h + 1, :]
                dec = jnp.exp(jnp.where(causal, seg, -jnp.inf))
                mm = (cb * dec * dt_t[h:h + 1, :]).astype(BF16)
                yh = jnp.dot(mm, xpb, preferred_element_type=F32)
                y_intra = yh if y_intra is None else jnp.where(lane_lo, y_intra, yh)
            sp = s_ref[p]
            y_inter = jnp.dot(cg, sp.astype(BF16), preferred_element_type=F32) * _pair_cols(ecum, p, lane_lo)
            ys.append(y_intra + y_inter + dsk_ref[:, p * LANE:(p + 1) * LANE] * xp)
            xw = (xp * _pair_cols(wdec, p, lane_lo)).astype(BF16)
            s_ref[p] = (_pair_cols(cdec, p, lane_lo1) * sp
                        + jnp.dot(bg_t, xw, preferred_element_type=F32))
    y = jnp.concatenate(ys, axis=1)
    y = y * _silu(z_ref[...])
    gw = d_inner // SSM_GROUPS
    outs = []
    for g in range(SSM_GROUPS):
        yg = y[:, g * gw:(g + 1) * gw]
        outs.append(yg * lax.rsqrt(jnp.mean(yg * yg, axis=-1, keepdims=True) + EPS))
    y_ref[...] = (jnp.concatenate(outs, axis=1) * nw_ref[...]).astype(y_ref.dtype)

    @pl.when(c == pl.num_programs(1) - 1)
    def _():
        st_ref[0] = s_ref[...]


def ssd_prompt(proj, conv_w, conv_b, dtb_pad, alog_pad, dskip_lane, norm_w, *, batch, seq, d_inner, col0):
    q = SSD_CHUNK
    nc = seq // q
    conv_dim = conv_w.shape[1]
    cw = conv_w.shape[0]
    n_pairs = d_inner // (2 * SSM_HEADDIM)
    zb = col0 // d_inner
    xb = (col0 + d_inner) // conv_dim
    db = (col0 + d_inner + conv_dim) // LANE
    kern = functools.partial(_ssd_kernel, d_inner=d_inner, n_pairs=n_pairs, conv_w=cw)
    const = lambda b, c: (0, 0)
    return pl.pallas_call(
        kern,
        grid=(batch, nc),
        in_specs=[pl.BlockSpec((q, d_inner), lambda b, c: (b * nc + c, zb)),
                  pl.BlockSpec((q, conv_dim), lambda b, c: (b * nc + c, xb)),
                  pl.BlockSpec((q, LANE), lambda b, c: (b * nc + c, db)),
                  pl.BlockSpec((cw, conv_dim), const),
                  pl.BlockSpec((1, conv_dim), const),
                  pl.BlockSpec((1, LANE), const),
                  pl.BlockSpec((1, LANE), const),
                  pl.BlockSpec((1, d_inner), const),
                  pl.BlockSpec((1, d_inner), const)],
        out_specs=[pl.BlockSpec((q, d_inner), lambda b, c: (b * nc + c, 0)),
                   pl.BlockSpec((1, n_pairs, D_STATE, LANE), lambda b, c: (b, 0, 0, 0))],
        out_shape=[jax.ShapeDtypeStruct((batch * seq, d_inner), BF16),
                   jax.ShapeDtypeStruct((batch, n_pairs, D_STATE, LANE), F32)],
        scratch_shapes=[pltpu.VMEM((SUBLANE + q, conv_dim), F32),
                        pltpu.VMEM((n_pairs, D_STATE, LANE), F32)],
        compiler_params=_cp(("parallel", "arbitrary")),
        name="ssd_prompt",
    )(proj, proj, proj, conv_w, conv_b, dtb_pad, alog_pad, dskip_lane, norm_w)


def _state_from_pairs(st):
    b, npair, n, _ = st.shape
    return st.reshape(b, npair, n, 2, SSM_HEADDIM).transpose(0, 1, 3, 4, 2).reshape(b, 2 * npair, SSM_HEADDIM, n)


def _diff_lambda(lamp, lam_init):
    l1 = jnp.sum(lamp[0:1] * lamp[1:2], axis=-1, keepdims=True)
    l2 = jnp.sum(lamp[2:3] * lamp[3:4], axis=-1, keepdims=True)
    return jnp.exp(l1) - jnp.exp(l2) + lam_init


def _diff_finish(acc0, l0, acc1, l1, lam, sw, lam_init):
    o = acc0 / l0 - lam * (acc1 / l1)
    o = o * lax.rsqrt(jnp.mean(o * o, axis=-1, keepdims=True) + EPS) * sw
    return o * (1.0 - lam_init)


def _diff_attn_kernel(lamp_ref, sw_ref, q_ref, k_ref, v_ref, o_ref, m_ref, l_ref, acc_ref, *, t, scale, lam_init):
    qi = pl.program_id(2)
    q = (q_ref[...].astype(F32) * scale).astype(BF16)
    m_ref[...] = jnp.full_like(m_ref, NEG_BIG)
    l_ref[...] = jnp.zeros_like(l_ref)
    acc_ref[...] = jnp.zeros_like(acc_ref)
    row = lax.broadcasted_iota(jnp.int32, (t, t), 0)
    col = lax.broadcasted_iota(jnp.int32, (t, t), 1)

    def step(kb, masked):
        start = pl.multiple_of(kb * t, t)
        k = k_ref[pl.ds(start, t), :]
        v = v_ref[pl.ds(start, t), :]
        for m in range(2):
            s = lax.dot_general(q[:, m * HD_C:(m + 1) * HD_C], k[:, m * HD_C:(m + 1) * HD_C], NT_DIMS,
                                preferred_element_type=F32)
            if masked:
                s = jnp.where(row >= col, s, NEG_BIG)
            m_old = m_ref[m]
            m_new = jnp.maximum(m_old, jnp.max(s, axis=-1, keepdims=True))
            alpha = jnp.exp(m_old - m_new)
            p = jnp.exp(s - m_new)
            l_ref[m] = alpha * l_ref[m] + jnp.sum(p, axis=-1, keepdims=True)
            acc_ref[m] = alpha * acc_ref[m] + jnp.dot(p.astype(BF16), v, preferred_element_type=F32)
            m_ref[m] = m_new

    def body(kb, carry):
        step(kb, False)
        return carry

    lax.fori_loop(0, qi, body, 0)
    step(qi, True)
    lam = _diff_lambda(lamp_ref[...], lam_init)
    o_ref[...] = _diff_finish(acc_ref[0], l_ref[0], acc_ref[1], l_ref[1], lam, sw_ref[...], lam_init).astype(o_ref.dtype)


def diff_attention_prompt(proj_bf, lamp, subln_w, *, batch, seq, heads, t, lam_init):
    hw = 2 * HD_C
    nq = seq // t
    kern = functools.partial(_diff_attn_kernel, t=t, scale=1.0 / math.sqrt(HD_C), lam_init=lam_init)
    return pl.pallas_call(
        kern,
        grid=(batch, heads, nq),
        in_specs=[pl.BlockSpec((4, HD_C), lambda b, h, i: (0, 0)),
                  pl.BlockSpec((1, hw), lambda b, h, i: (0, 0)),
                  pl.BlockSpec((t, hw), lambda b, h, i: (b * nq + i, h)),
                  pl.BlockSpec((seq, hw), lambda b, h, i: (b, heads + h)),
                  pl.BlockSpec((seq, hw), lambda b, h, i: (b, 2 * heads + h))],
        out_specs=pl.BlockSpec((t, hw), lambda b, h, i: (b * nq + i, h)),
        out_shape=jax.ShapeDtypeStruct((batch * seq, heads * hw), BF16),
        scratch_shapes=[pltpu.VMEM((2, t, 1), F32), pltpu.VMEM((2, t, 1), F32), pltpu.VMEM((2, t, hw), F32)],
        compiler_params=_cp(("parallel", "parallel", "arbitrary")),
        name="diff_attention",
    )(lamp, subln_w, proj_bf, proj_bf, proj_bf)


def _diff_sample_kernel(pt_ref, q_ref, kn_ref, vn_ref, lamp_ref, sw_ref, *rest, pp, scale, lam_init):
    del pt_ref
    k_refs = rest[:pp]
    v_refs = rest[pp:2 * pp]
    o_ref, m_ref, l_ref, acc_ref = rest[2 * pp:]
    pg = pl.program_id(1)
    q = q_ref[0] * scale

    @pl.when(pg == 0)
    def _():
        prod = q * kn_ref[0]
        for m in range(2):
            m_ref[m] = jnp.sum(prod[:, m * HD_C:(m + 1) * HD_C], axis=-1, keepdims=True)
            l_ref[m] = jnp.ones_like(l_ref[m])
            acc_ref[m] = vn_ref[0]

    for i in range(pp):
        kk = k_refs[i][0, 0]
        vv = v_refs[i][0, 0]
        prod = kk * q[None]
        for m in range(2):
            s = jnp.sum(prod[:, :, m * HD_C:(m + 1) * HD_C], axis=-1, keepdims=True)
            m_old = m_ref[m]
            m_new = jnp.maximum(m_old, jnp.max(s, axis=0))
            alpha = jnp.exp(m_old - m_new)
            p = jnp.exp(s - m_new[None])
            l_ref[m] = alpha * l_ref[m] + jnp.sum(p, axis=0)
            acc_ref[m] = alpha * acc_ref[m] + jnp.sum(p * vv, axis=0)
            m_ref[m] = m_new

    @pl.when(pg == pl.num_programs(1) - 1)
    def _():
        lam = _diff_lambda(lamp_ref[...], lam_init)
        o_ref[0] = _diff_finish(acc_ref[0], l_ref[0], acc_ref[1], l_ref[1], lam, sw_ref[...], lam_init)


def diff_attention_sample(page_table, q3, kn3, vn3, lamp, subln_w, cache_k, cache_v, *, layer, pp, lam_init):
    bsz, heads, hw = q3.shape
    n_pages = page_table.shape[1]
    page = cache_k.shape[2]
    row_spec = pl.BlockSpec((1, heads, hw), lambda b, p, pt: (b, 0, 0))

    def page_spec(i):
        return pl.BlockSpec((1, 1, page, heads, hw), lambda b, p, pt: (layer, pt[b, p * pp + i], 0, 0, 0))

    kern = functools.partial(_diff_sample_kernel, pp=pp, scale=1.0 / math.sqrt(HD_C), lam_init=lam_init)
    grid_spec = pltpu.PrefetchScalarGridSpec(
        num_scalar_prefetch=1,
        grid=(bsz, n_pages // pp),
        in_specs=[row_spec, row_spec, row_spec,
                  pl.BlockSpec((4, HD_C), lambda b, p, pt: (0, 0)),
                  pl.BlockSpec((1, hw), lambda b, p, pt: (0, 0))]
                 + [page_spec(i) for i in range(pp)] + [page_spec(i) for i in range(pp)],
        out_specs=row_spec,
        scratch_shapes=[pltpu.VMEM((2, heads, 1), F32), pltpu.VMEM((2, heads, 1), F32),
                        pltpu.VMEM((2, heads, hw), F32)],
    )
    return pl.pallas_call(
        kern,
        grid_spec=grid_spec,
        out_shape=jax.ShapeDtypeStruct((bsz, heads, hw), F32),
        compiler_params=_cp(("parallel", "arbitrary")),
        name="diff_attention_sample",
    )(page_table, q3, kn3, vn3, lamp, subln_w, *([cache_k] * pp), *([cache_v] * pp))


def _win_sample_kernel(q_ref, kn_ref, vn_ref, ck_ref, cv_ref, o_ref, m_ref, l_ref, acc_ref, *, tr, wb, scale):
    r = pl.program_id(1)
    q = q_ref[0] * scale

    @pl.when(r == 0)
    def _():
        m_ref[...] = jnp.sum(q * kn_ref[0], axis=-1, keepdims=True)
        l_ref[...] = jnp.full_like(l_ref, float(len(DILATED_PATTERNS)))
        acc_ref[...] = float(len(DILATED_PATTERNS)) * vn_ref[0]

    kk = ck_ref[0]
    s = jnp.sum(kk * q[None], axis=-1, keepdims=True)
    j = r * tr + lax.broadcasted_iota(jnp.int32, s.shape, 0)
    d = wb - j
    mult = jnp.zeros(s.shape, F32)
    for win, dil in DILATED_PATTERNS:
        steps = win // dil
        mult = mult + ((d % dil == 0) & (d <= steps * dil)).astype(F32)
    s = jnp.where(mult > 0.0, s, NEG_BIG)
    m_old = m_ref[...]
    m_new = jnp.maximum(m_old, jnp.max(s, axis=0))
    alpha = jnp.exp(m_old - m_new)
    p = jnp.exp(s - m_new[None]) * mult
    l_ref[...] = alpha * l_ref[...] + jnp.sum(p, axis=0)
    acc_ref[...] = alpha * acc_ref[...] + jnp.sum(p * cv_ref[0], axis=0)
    m_ref[...] = m_new

    @pl.when(r == pl.num_programs(1) - 1)
    def _():
        o_ref[0] = acc_ref[...] / l_ref[...]


def dilated_attention_sample(q3, kn3, vn3, cache_k, cache_v, *, tr):
    bsz, heads, hd = q3.shape
    wb = cache_k.shape[1]
    row_spec = pl.BlockSpec((1, heads, hd), lambda b, r: (b, 0, 0))
    c_spec = pl.BlockSpec((1, tr, heads, hd), lambda b, r: (b, r, 0, 0))
    kern = functools.partial(_win_sample_kernel, tr=tr, wb=wb, scale=1.0 / math.sqrt(hd))
    return pl.pallas_call(
        kern,
        grid=(bsz, wb // tr),
        in_specs=[row_spec, row_spec, row_spec, c_spec, c_spec],
        out_specs=row_spec,
        out_shape=jax.ShapeDtypeStruct((bsz, heads, hd), F32),
        scratch_shapes=[pltpu.VMEM((heads, 1), F32), pltpu.VMEM((heads, 1), F32), pltpu.VMEM((heads, hd), F32)],
        compiler_params=_cp(("parallel", "arbitrary")),
        name="dilated_attention_sample",
    )(q3, kn3, vn3, cache_k, cache_v)


def _lane_to_col(v):
    n = v.shape[1] // LANE
    return jnp.concatenate([jnp.broadcast_to(v[:, k * LANE:(k + 1) * LANE], (LANE, LANE)).T for k in range(n)], axis=0)


def _col_to_lane(c):
    n = c.shape[0] // LANE
    return jnp.concatenate([jnp.broadcast_to(c[k * LANE:(k + 1) * LANE], (LANE, LANE)).T[0:1] for k in range(n)], axis=1)


def _ssd_step_kernel(z_ref, xbc_ref, dt_ref, cs_ref, st_ref, cw_ref, cb_ref, dtb_ref, alog_ref, dsk_ref, nw_ref,
                     e_ref, y_ref, cso_ref, sto_ref, *, d_inner):
    x = xbc_ref[0]
    cs = cs_ref[0]
    ncs = cs.shape[0]
    conv = cb_ref[...] + x * cw_ref[ncs:ncs + 1, :]
    for kk in range(ncs):
        conv = conv + cs[kk:kk + 1, :] * cw_ref[kk:kk + 1, :]
    cso_ref[0, 0:ncs - 1, :] = cs[1:ncs, :]
    cso_ref[0, ncs - 1:ncs, :] = x
    act = _silu(conv)
    gn = SSM_GROUPS * D_STATE
    xs = act[:, :d_inner]
    bm = act[:, d_inner:d_inner + gn]
    cm = act[:, d_inner + gn:d_inner + 2 * gn]
    dt = _softplus(dt_ref[0] + dtb_ref[...])
    dt_lane = jnp.dot(jnp.broadcast_to(dt, (SUBLANE, LANE)), e_ref[...], preferred_element_type=F32,
                      precision=lax.Precision.HIGHEST)[0:1]
    dec_lane = jnp.exp(dt_lane * (-jnp.exp(alog_ref[...])))
    dec_col = _lane_to_col(dec_lane)
    dtx_col = _lane_to_col(dt_lane * xs)
    rows_g = d_inner // SSM_GROUPS
    b_rows = jnp.concatenate([jnp.broadcast_to(bm[:, g * D_STATE:(g + 1) * D_STATE], (rows_g, D_STATE))
                              for g in range(SSM_GROUPS)], axis=0)
    c_rows = jnp.concatenate([jnp.broadcast_to(cm[:, g * D_STATE:(g + 1) * D_STATE], (rows_g, D_STATE))
                              for g in range(SSM_GROUPS)], axis=0)
    st = st_ref[0].reshape(d_inner, D_STATE)
    st_new = dec_col * st + dtx_col * b_rows
    sto_ref[0] = st_new.reshape(sto_ref.shape[1:])
    y = _col_to_lane(jnp.sum(st_new * c_rows, axis=-1, keepdims=True))
    y = y + dsk_ref[...] * xs
    y = y * _silu(z_ref[0])
    gw = d_inner // SSM_GROUPS
    outs = []
    for g in range(SSM_GROUPS):
        yg = y[:, g * gw:(g + 1) * gw]
        outs.append(yg * lax.rsqrt(jnp.mean(yg * yg, axis=-1, keepdims=True) + EPS))
    y_ref[0] = jnp.concatenate(outs, axis=1) * nw_ref[...]


def ssd_step(z3, xbc3, dt3, conv_state, ssm_state, conv_w, conv_b, dtb_pad, alog_lane, dskip_lane, norm_w, expand):
    bsz, _, d_inner = z3.shape
    conv_dim = xbc3.shape[2]
    ncs = conv_state.shape[1]
    hb, hp, n = ssm_state.shape[1:]
    const = lambda b: (0, 0)
    kern = functools.partial(_ssd_step_kernel, d_inner=d_inner)
    return pl.pallas_call(
        kern,
        grid=(bsz,),
        in_specs=[pl.BlockSpec((1, 1, d_inner), lambda b: (b, 0, 0)),
                  pl.BlockSpec((1, 1, conv_dim), lambda b: (b, 0, 0)),
                  pl.BlockSpec((1, 1, LANE), lambda b: (b, 0, 0)),
                  pl.BlockSpec((1, ncs, conv_dim), lambda b: (b, 0, 0)),
                  pl.BlockSpec((1, hb, hp, n), lambda b: (b, 0, 0, 0)),
                  pl.BlockSpec((ncs + 1, conv_dim), const),
                  pl.BlockSpec((1, conv_dim), const),
                  pl.BlockSpec((1, LANE), const),
                  pl.BlockSpec((1, d_inner), const),
                  pl.BlockSpec((1, d_inner), const),
                  pl.BlockSpec((1, d_inner), const),
                  pl.BlockSpec((LANE, d_inner), const)],
        out_specs=[pl.BlockSpec((1, 1, d_inner), lambda b: (b, 0, 0)),
                   pl.BlockSpec((1, ncs, conv_dim), lambda b: (b, 0, 0)),
                   pl.BlockSpec((1, hb, hp, n), lambda b: (b, 0, 0, 0))],
        out_shape=[jax.ShapeDtypeStruct((bsz, 1, d_inner), F32),
                   jax.ShapeDtypeStruct(conv_state.shape, F32),
                   jax.ShapeDtypeStruct(ssm_state.shape, F32)],
        compiler_params=_cp(("parallel",)),
        name="ssd_step",
    )(z3, xbc3, dt3, conv_state, ssm_state, conv_w, conv_b, dtb_pad, alog_lane, dskip_lane, norm_w, expand)


PROMPT_TM = 512
ROUTE_TM = 256
EXPERT_NI = 8
ATTN_T = 512
SAMPLE_PAD = LANE
WIN_SAMPLE_ROWS = 512
PAGES_PER_STEP = 4


def _pad_cols(w, n):
    return jnp.pad(w, ((0, 0), (0, n - w.shape[1])))


def _pad_rows(a, n):
    return jnp.pad(a, ((0, n - a.shape[0]),) + ((0, 0),) * (a.ndim - 1))


def _round_up(n, m):
    return -(-n // m) * m


def kernel(x_prompt, x_sample, c_prompt, c_sample, cache_win_k, cache_win_v, state_conv, state_ssm,
           cache_diff_k, cache_diff_v, page_table, norm1_w, norm2_w, w_ada, b_ada, w_in_even, w_out_even,
           conv_w, conv_b, dt_bias, a_log, d_skip, ssm_norm_w, w_in_odd, w_out_odd, lambda_q1, lambda_k1,
           lambda_q2, lambda_k2, subln_w, peer_wq, peer_keys, peer_u, peer_v, final_norm_w):
    bsz, seq, d = x_prompt.shape
    sb, st, _ = x_sample.shape
    assert st == 1, "one new token per sample sequence"
    depth = w_ada.shape[0]
    h_a = cache_win_k.shape[3]
    a_width = h_a * HD_A
    d_inner = ssm_norm_w.shape[1]
    conv_dim = conv_w.shape[2]
    h_b = dt_bias.shape[1]
    h_c = cache_diff_k.shape[3]
    c_width = h_c * 2 * HD_C
    wb_p = min(max(w for w, _ in DILATED_PATTERNS), seq)
    ncs = conv_w.shape[1] - 1

    xp = x_prompt.reshape(bsz * seq, d)
    xs = x_sample.reshape(sb, d)

    n_c = _round_up(bsz + sb, SUBLANE)
    c_all = _pad_rows(jnp.concatenate([c_prompt, c_sample], axis=0), n_c)
    mod = adaln_all(c_all, w_ada, b_ada)

    expand = (jnp.arange(LANE)[:, None] == (jnp.arange(d_inner) // SSM_HEADDIM)[None, :]).astype(F32)

    outs = {k: [] for k in ("pwk", "pwv", "pcv", "pss", "pdk", "pdv", "swk", "swv", "scv", "sss", "sdk", "sdv")}
    for l in range(depth):
        parts = [mod[l, :, i * d:(i + 1) * d] for i in range(6)]
        pm = [p[:bsz][:, None, :] for p in parts]
        sm = [p[bsz:bsz + sb][None] for p in parts]
        n1 = norm1_w[l][None]
        n2 = norm2_w[l][None]
        if l % 2 == 0:
            e = l // 2
            even_in = w_in_even.shape[2]
            n_pad = _round_up(even_in, 7 * LANE)
            w_in = _pad_cols(w_in_even[e], n_pad).astype(BF16)
            tn = n_pad // 7
            wo = w_out_even[e].astype(BF16)
            dtb_pad = _pad_cols(dt_bias[e][None], LANE)
            alog_pad = _pad_cols(a_log[e][None], LANE)
            alog_lane = jnp.repeat(a_log[e], SSM_HEADDIM)[None]
            dsk_lane = jnp.repeat(d_skip[e], SSM_HEADDIM)[None]
            cw, cb, nw = conv_w[e], conv_b[e][None], ssm_norm_w[e][None]
            col_z = 3 * a_width

            proj, proj_bf = normmod_matmul(xp, n1, pm[0], pm[1], w_in, tm=PROMPT_TM, tn=tn, rows_per_group=seq)
            ya = dilated_attention_prompt(proj_bf, batch=bsz, seq=seq, heads=h_a, t=ATTN_T)
            yb, st_pairs = ssd_prompt(proj, cw, cb, dtb_pad, alog_pad, dsk_lane, nw,
                                      batch=bsz, seq=seq, d_inner=d_inner, col0=col_z)
            xp = matmul_residual([ya, yb], [wo[:a_width], wo[a_width:]], xp, pm[2],
                                 tm=PROMPT_TM, tn=1024, rows_per_group=seq)
            proj3 = proj.reshape(bsz, seq, n_pad)
            outs["pwk"].append(proj3[:, seq - wb_p:, a_width:2 * a_width].reshape(bsz, wb_p, h_a, HD_A))
            outs["pwv"].append(proj3[:, seq - wb_p:, 2 * a_width:3 * a_width].reshape(bsz, wb_p, h_a, HD_A))
            outs["pcv"].append(proj3[:, seq - ncs:, col_z + d_inner:col_z + d_inner + conv_dim])
            outs["pss"].append(_state_from_pairs(st_pairs))

            sproj, _ = normmod_matmul(xs, n1, sm[0], sm[1], w_in, tm=sb, tn=tn, rows_per_group=sb)
            q3 = sproj[:, :a_width].reshape(sb, h_a, HD_A)
            k3 = sproj[:, a_width:2 * a_width].reshape(sb, h_a, HD_A)
            v3 = sproj[:, 2 * a_width:3 * a_width].reshape(sb, h_a, HD_A)
            ya_s = dilated_attention_sample(q3, k3, v3, cache_win_k[e], cache_win_v[e],
                                            tr=min(WIN_SAMPLE_ROWS, cache_win_k.shape[2]))
            z3 = sproj[:, col_z:col_z + d_inner][:, None, :]
            xbc3 = sproj[:, col_z + d_inner:col_z + d_inner + conv_dim][:, None, :]
            dt3 = sproj[:, col_z + d_inner + conv_dim:col_z + d_inner + conv_dim + LANE][:, None, :]
            yb_s, cs_new, ss_new = ssd_step(z3, xbc3, dt3, state_conv[e], state_ssm[e], cw, cb, dtb_pad,
                                            alog_lane, dsk_lane, nw, expand)
            xs = matmul_residual([ya_s.reshape(sb, a_width).astype(BF16), yb_s.reshape(sb, d_inner).astype(BF16)],
                                 [wo[:a_width], wo[a_width:]], xs, sm[2], tm=sb, tn=1024, rows_per_group=sb)
            outs["swk"].append(k3[:, None])
            outs["swv"].append(v3[:, None])
            outs["scv"].append(cs_new)
            outs["sss"].append(ss_new)
        else:
            o_ = l // 2
            lam_init = 0.8 - 0.6 * math.exp(-0.3 * l)
            w_in = w_in_odd[o_].astype(BF16)
            wo = w_out_odd[o_].astype(BF16)
            lamp = jnp.stack([lambda_q1[o_], lambda_k1[o_], lambda_q2[o_], lambda_k2[o_]])
            sw = subln_w[o_][None]

            proj, proj_bf = normmod_matmul(xp, n1, pm[0], pm[1], w_in, tm=PROMPT_TM, tn=1024, rows_per_group=seq)
            o = diff_attention_prompt(proj_bf, lamp, sw, batch=bsz, seq=seq, heads=h_c, t=ATTN_T, lam_init=lam_init)
            xp = matmul_residual([o], [wo], xp, pm[2], tm=PROMPT_TM, tn=1024, rows_per_group=seq)
            outs["pdk"].append(proj[:, c_width:2 * c_width].reshape(bsz, seq, h_c, 2 * HD_C))
            outs["pdv"].append(proj[:, 2 * c_width:3 * c_width].reshape(bsz, seq, h_c, 2 * HD_C))

            sproj, _ = normmod_matmul(xs, n1, sm[0], sm[1], w_in, tm=sb, tn=1024, rows_per_group=sb)
            q3 = sproj[:, :c_width].reshape(sb, h_c, 2 * HD_C)
            k3 = sproj[:, c_width:2 * c_width].reshape(sb, h_c, 2 * HD_C)
            v3 = sproj[:, 2 * c_width:3 * c_width].reshape(sb, h_c, 2 * HD_C)
            o_s = diff_attention_sample(page_table, q3, k3, v3, lamp, sw, cache_diff_k, cache_diff_v,
                                        layer=o_, pp=PAGES_PER_STEP, lam_init=lam_init)
            xs = matmul_residual([o_s.reshape(sb, c_width).astype(BF16)], [wo], xs, sm[2],
                                 tm=sb, tn=1024, rows_per_group=sb)
            outs["sdk"].append(k3[:, None])
            outs["sdv"].append(v3[:, None])

        wqt = _split_bf16(peer_wq[l].T)
        nhc = peer_keys.shape[1] * peer_keys.shape[2]
        keys2 = _split_bf16(peer_keys[l].reshape(nhc, peer_keys.shape[3], peer_keys.shape[4]))
        u = peer_u[l].astype(BF16)
        vt = peer_v[l].T.astype(BF16)
        routed = peer_route(xp, n2, pm[3], pm[4], wqt, keys2, tm=ROUTE_TM, rows_per_group=seq)
        xp = peer_experts(*routed, u, vt, xp, pm[5], tm=PROMPT_TM, ni=EXPERT_NI, rows_per_group=seq)

        xs_pad = _pad_rows(xs, SAMPLE_PAD)
        smp = [jnp.pad(m_, ((0, 0), (0, SAMPLE_PAD - sb), (0, 0))) for m_ in sm[3:6]]
        routed = peer_route(xs_pad, n2, smp[0], smp[1], wqt, keys2, tm=SAMPLE_PAD, rows_per_group=SAMPLE_PAD)
        xs = peer_experts(*routed, u, vt, xs_pad, smp[2], tm=SAMPLE_PAD, ni=EXPERT_NI,
                          rows_per_group=SAMPLE_PAD)[:sb]

    y_prompt = rmsnorm_rows(xp, final_norm_w[None], tm=PROMPT_TM).reshape(bsz, seq, d)
    y_sample = rmsnorm_rows(xs, final_norm_w[None], tm=sb).reshape(sb, st, d)
    stk = lambda k: jnp.stack(outs[k])
    return (y_prompt, y_sample, stk("pwk"), stk("pwv"), stk("pcv"), stk("pss"), stk("pdk"), stk("pdv"),
            stk("swk"), stk("swv"), stk("scv"), stk("sss"), stk("sdk"), stk("sdv"))
```

```python
import functools
import math

import jax
import jax.numpy as jnp
from jax import lax
from jax.experimental import pallas as pl
from jax.experimental.pallas import tpu as pltpu

F32 = jnp.float32
BF16 = jnp.bfloat16
EPS = 1e-6
NEG_BIG = -1e30

HD_A = 128
HD_C = 128
DILATED_PATTERNS = ((128, 1), (512, 4), (2048, 16))
SSM_HEADDIM = 64
SSM_GROUPS = 4
D_STATE = 128
SSD_CHUNK = 128
PEER_HEADS = 8
PEER_TOPK = 16

V7X_VMEM_LIMIT = 58 * 1024 * 1024
LANE = 128
SUBLANE = 8
BF16_ROWS = 16

NT_DIMS = (((1,), (1,)), ((), ()))


def _cp(sem, vmem=V7X_VMEM_LIMIT):
    return pltpu.CompilerParams(dimension_semantics=sem, vmem_limit_bytes=vmem)


def _sigmoid(x):
    return 1.0 / (1.0 + jnp.exp(-x))


def _silu(x):
    return x * _sigmoid(x)


def _softplus(x):
    return jnp.maximum(x, 0.0) + jnp.log1p(jnp.exp(-jnp.abs(x)))


def _norm_modulate(x, nw, sh, sc):
    ms = jnp.mean(x * x, axis=-1, keepdims=True)
    y = x * lax.rsqrt(ms + EPS) * nw
    return y * (1.0 + sc) + sh


def _adaln_kernel(c_ref, w_ref, b_ref, o_ref):
    a_hi, a_lo = _split_bf16(_silu(c_ref[...]))
    w_hi, w_lo = _split_bf16(w_ref[0])
    o_ref[0] = _dot3(a_hi, a_lo, w_hi, w_lo, NN_DIMS) + b_ref[0]


def adaln_all(c_all, w_ada, b_ada):
    nl, d, n = w_ada.shape
    r = c_all.shape[0]
    tn = 1024
    return pl.pallas_call(
        _adaln_kernel,
        grid=(nl, n // tn),
        in_specs=[pl.BlockSpec((r, d), lambda l, j: (0, 0)),
                  pl.BlockSpec((1, d, tn), lambda l, j: (l, 0, j)),
                  pl.BlockSpec((1, 1, tn), lambda l, j: (l, 0, j))],
        out_specs=pl.BlockSpec((1, r, tn), lambda l, j: (l, 0, j)),
        out_shape=jax.ShapeDtypeStruct((nl, r, n), F32),
        compiler_params=_cp(("parallel", "parallel")),
        name="adaln",
    )(c_all, w_ada, b_ada.reshape(nl, 1, n))


def _normmod_mm_kernel(x_ref, nw_ref, sh_ref, sc_ref, w_ref, o_ref, ob_ref, h_ref):
    @pl.when(pl.program_id(1) == 0)
    def _():
        h_ref[...] = _norm_modulate(x_ref[...], nw_ref[...], sh_ref[0], sc_ref[0]).astype(BF16)

    o = jnp.dot(h_ref[...], w_ref[...], preferred_element_type=F32)
    o_ref[...] = o
    ob_ref[...] = o.astype(BF16)


def normmod_matmul(x, nw, sh, sc, w, *, tm, tn, rows_per_group):
    t, d = x.shape
    n = w.shape[1]
    r = sh.shape[1]
    tpg = rows_per_group // tm
    mod_spec = pl.BlockSpec((1, r, d), lambda i, j: (i // tpg, 0, 0))
    return pl.pallas_call(
        _normmod_mm_kernel,
        grid=(t // tm, n // tn),
        in_specs=[pl.BlockSpec((tm, d), lambda i, j: (i, 0)),
                  pl.BlockSpec((1, d), lambda i, j: (0, 0)),
                  mod_spec, mod_spec,
                  pl.BlockSpec((d, tn), lambda i, j: (0, j))],
        out_specs=[pl.BlockSpec((tm, tn), lambda i, j: (i, j)),
                   pl.BlockSpec((tm, tn), lambda i, j: (i, j))],
        out_shape=[jax.ShapeDtypeStruct((t, n), F32), jax.ShapeDtypeStruct((t, n), BF16)],
        scratch_shapes=[pltpu.VMEM((tm, d), BF16)],
        compiler_params=_cp(("parallel", "arbitrary")),
        name="normmod_matmul",
    )(x, nw, sh, sc, w)


def _mm_res_kernel(*refs, n_pairs):
    a_refs = refs[:n_pairs]
    w_refs = refs[n_pairs:2 * n_pairs]
    x_ref, g_ref, o_ref = refs[2 * n_pairs:]
    acc = jnp.dot(a_refs[0][...], w_refs[0][...], preferred_element_type=F32)
    for a_ref, w_ref in zip(a_refs[1:], w_refs[1:]):
        acc = acc + jnp.dot(a_ref[...], w_ref[...], preferred_element_type=F32)
    o_ref[...] = x_ref[...] + g_ref[0] * acc


def matmul_residual(a_list, w_list, x, gate, *, tm, tn, rows_per_group):
    t, n = x.shape
    r = gate.shape[1]
    tpg = rows_per_group // tm
    npairs = len(a_list)
    in_specs = ([pl.BlockSpec((tm, a.shape[1]), lambda i, j: (i, 0)) for a in a_list]
                + [pl.BlockSpec((w.shape[0], tn), lambda i, j: (0, j)) for w in w_list]
                + [pl.BlockSpec((tm, tn), lambda i, j: (i, j)),
                   pl.BlockSpec((1, r, tn), lambda i, j: (i // tpg, 0, j))])
    return pl.pallas_call(
        functools.partial(_mm_res_kernel, n_pairs=npairs),
        grid=(t // tm, n // tn),
        in_specs=in_specs,
        out_specs=pl.BlockSpec((tm, tn), lambda i, j: (i, j)),
        out_shape=jax.ShapeDtypeStruct((t, n), F32),
        compiler_params=_cp(("parallel", "parallel")),
        name="matmul_residual",
    )(*a_list, *w_list, x, gate)


def _rms_kernel(x_ref, w_ref, o_ref):
    x = x_ref[...]
    ms = jnp.mean(x * x, axis=-1, keepdims=True)
    o_ref[...] = x * lax.rsqrt(ms + EPS) * w_ref[...]


def rmsnorm_rows(x, w, *, tm):
    t, d = x.shape
    return pl.pallas_call(
        _rms_kernel,
        grid=(t // tm,),
        in_specs=[pl.BlockSpec((tm, d), lambda i: (i, 0)), pl.BlockSpec((1, d), lambda i: (0, 0))],
        out_specs=pl.BlockSpec((tm, d), lambda i: (i, 0)),
        out_shape=jax.ShapeDtypeStruct((t, d), F32),
        compiler_params=_cp(("parallel",)),
        name="final_rmsnorm",
    )(x, w)


def _top_values_ranked(s, k):
    vals = []
    rank = jnp.full(s.shape, float(k), F32)
    for it in range(k):
        m = jnp.max(s, axis=0, keepdims=True)
        hit = s == m
        rank = jnp.where(hit, float(it), rank)
        vals.append(m)
        if it + 1 < k:
            s = jnp.where(hit, -jnp.inf, s)
    return vals, rank


def _split_bf16(a):
    hi = a.astype(BF16)
    lo = (a - hi.astype(F32)).astype(BF16)
    return hi, lo


def _dot3(a_hi, a_lo, b_hi, b_lo, dims):
    f = functools.partial(lax.dot_general, dimension_numbers=dims, preferred_element_type=F32)
    return f(a_hi, b_hi) + (f(a_hi, b_lo) + f(a_lo, b_hi))


NN_DIMS = (((1,), (0,)), ((), ()))


def _peer_route_kernel(x_ref, nw_ref, sh_ref, sc_ref, wqh_ref, wql_ref, kh_ref, kl_ref,
                       h_ref, cnt_ref, r_ref, rank_ref, e2_ref, qt_ref, *, heads, topk):
    h = _norm_modulate(x_ref[...], nw_ref[...], sh_ref[0], sc_ref[0])
    hb, hl = _split_bf16(h)
    h_ref[...] = hb
    qt_ref[...] = _dot3(wqh_ref[0], wql_ref[0], hb, hl, NT_DIMS)
    nk = kh_ref.shape[2]
    tm = hb.shape[0]
    half = topk // 2
    row8 = lax.broadcasted_iota(jnp.int32, (half, tm), 0)

    def head_body(hd, carry):
        s = []
        for c in range(2):
            idx = hd * 2 + c
            qh, ql = _split_bf16(qt_ref[pl.ds(pl.multiple_of(idx * nk, nk), nk), :])
            s.append(_dot3(kh_ref[0, idx], kl_ref[0, idx], qh, ql, NN_DIMS))
        v1, rank1 = _top_values_ranked(s[0], topk)
        v2, rank2 = _top_values_ranked(s[1], topk)
        v1s = jnp.concatenate(v1, axis=0)
        v2s = jnp.concatenate(v2, axis=0)
        pieces = [v1[0] + v2s, v1[1] + v2s[0:half]]
        for a in range(2, half):
            pieces.append(jnp.where(row8 < topk // (a + 1), v1[a] + v2s[0:half], -jnp.inf))
        pieces.append(v1s[half:topk] + v2[0])
        cand = jnp.concatenate(pieces, axis=0)
        mx = v1[0] + v2[0]
        z = jnp.zeros_like(mx)
        tau = mx
        for it in range(topk):
            tau = jnp.max(cand, axis=0, keepdims=True)
            z = z + jnp.exp(tau - mx)
            if it + 1 < topk:
                cand = jnp.where(cand == tau, -jnp.inf, cand)
        cnt = jnp.zeros(s[0].shape, F32)
        for b in range(half):
            cnt = cnt + jnp.where(s[0] + v2[b] >= tau, 1.0, 0.0)
        extra = jnp.zeros_like(tau)
        for b in range(half, topk):
            extra = extra + jnp.where(v1[0] + v2[b] >= tau, 1.0, 0.0)
        cnt_ref[hd] = cnt + jnp.where(rank1 == 0.0, extra, 0.0)
        r_ref[hd] = jnp.exp(s[0] - v1[0]) / z
        rank_ref[hd] = rank2.astype(BF16)
        e2_ref[hd] = jnp.exp(s[1] - v2[0]).astype(BF16)
        return carry

    lax.fori_loop(0, heads, head_body, 0)


def peer_route(x, nw, sh, sc, wqt, keys2, layer, *, tm, rows_per_group):
    t, d = x.shape
    r = sh.shape[1]
    nq = wqt[0].shape[1]
    _, nhc, nk, dh = keys2[0].shape
    heads = nhc // 2
    tpg = rows_per_group // tm
    mod_spec = pl.BlockSpec((1, r, d), lambda i: (i // tpg, 0, 0))
    fac_spec = pl.BlockSpec((heads, nk, tm), lambda i: (0, 0, i))
    f32_shape = jax.ShapeDtypeStruct((heads, nk, t), F32)
    bf_shape = jax.ShapeDtypeStruct((heads, nk, t), BF16)
    wq_spec = pl.BlockSpec((1, nq, d), lambda i: (layer, 0, 0), pipeline_mode=pl.Buffered(1))
    key_spec = pl.BlockSpec((1, nhc, nk, dh), lambda i: (layer, 0, 0, 0), pipeline_mode=pl.Buffered(1))
    return pl.pallas_call(
        functools.partial(_peer_route_kernel, heads=heads, topk=PEER_TOPK),
        grid=(t // tm,),
        in_specs=[pl.BlockSpec((tm, d), lambda i: (i, 0)),
                  pl.BlockSpec((1, d), lambda i: (0, 0)),
                  mod_spec, mod_spec, wq_spec, wq_spec, key_spec, key_spec],
        out_specs=[pl.BlockSpec((tm, d), lambda i: (i, 0)), fac_spec, fac_spec, fac_spec, fac_spec],
        out_shape=[jax.ShapeDtypeStruct((t, d), BF16), f32_shape, f32_shape, bf_shape, bf_shape],
        scratch_shapes=[pltpu.VMEM((nq, tm), F32)],
        compiler_params=_cp(("parallel",)),
        name="peer_route",
    )(x, nw, sh, sc, wqt[0], wqt[1], keys2[0], keys2[1])


def _gelu_exact(a):
    return 0.5 * a * (1.0 + lax.erf(a * (1.0 / math.sqrt(2.0))))


def _peer_expert_kernel(h_ref, cnt_ref, r_ref, rank_ref, e2_ref, u_ref, vt_ref, x_ref, g_ref,
                        o_ref, acc_ref, wa_ref, *, heads, ni, sub):
    j = pl.program_id(1)
    nk = rank_ref.shape[1] * rank_ref.shape[2]

    @pl.when(j == 0)
    def _():
        acc_ref[...] = jnp.zeros_like(acc_ref)

    hb = h_ref[...]
    tm = hb.shape[0]
    pk = BF16_ROWS
    for c in range(ni // sub):
        lo, hi = c * sub * nk, (c + 1) * sub * nk
        act = lax.dot_general(u_ref[0, lo:hi, :], hb, NT_DIMS, preferred_element_type=F32)
        act = _gelu_exact(act).astype(BF16)
        gs = []
        for ii in range(c * sub, (c + 1) * sub):
            g = None
            for hd in range(heads):
                cnt = jnp.broadcast_to(cnt_ref[hd, ii:ii + 1, :], (pk, tm)).astype(BF16)[None]
                rr = jnp.broadcast_to(r_ref[hd, ii:ii + 1, :], (pk, tm)).astype(BF16)[None]
                contrib = jnp.where(rank_ref[hd] < cnt, e2_ref[hd] * rr, jnp.zeros((), BF16))
                g = contrib if g is None else g + contrib
            gs.append(g.reshape(nk, tm))
        wa_ref[lo:hi, :] = jnp.concatenate(gs, axis=0) * act
    acc_ref[...] += jnp.dot(vt_ref[0], wa_ref[...], preferred_element_type=F32)

    @pl.when(j == pl.num_programs(1) - 1)
    def _():
        o_ref[...] = x_ref[...] + g_ref[0] * acc_ref[...].T


def peer_experts(h, cnt, rfac, rank, e2, u, vt, layer, x, gate, *, tm, ni, sub, rows_per_group):
    t, d = x.shape
    heads, nk, _ = cnt.shape
    r = gate.shape[1]
    tpg = rows_per_group // tm
    te = ni * nk
    row_spec = pl.BlockSpec((heads, ni, tm), lambda i, j: (0, j, i))
    full_spec = pl.BlockSpec((heads, nk // BF16_ROWS, BF16_ROWS, tm), lambda i, j: (0, 0, 0, i))
    rank = rank.reshape(heads, nk // BF16_ROWS, BF16_ROWS, t)
    e2 = e2.reshape(heads, nk // BF16_ROWS, BF16_ROWS, t)
    return pl.pallas_call(
        functools.partial(_peer_expert_kernel, heads=heads, ni=ni, sub=sub),
        grid=(t // tm, nk // ni),
        in_specs=[pl.BlockSpec((tm, d), lambda i, j: (i, 0)),
                  row_spec, row_spec, full_spec, full_spec,
                  pl.BlockSpec((1, te, d), lambda i, j: (layer, j, 0)),
                  pl.BlockSpec((1, d, te), lambda i, j: (layer, 0, j)),
                  pl.BlockSpec((tm, d), lambda i, j: (i, 0), pipeline_mode=pl.Buffered(1)),
                  pl.BlockSpec((1, r, d), lambda i, j: (i // tpg, 0, 0))],
        out_specs=pl.BlockSpec((tm, d), lambda i, j: (i, 0)),
        out_shape=jax.ShapeDtypeStruct((t, d), F32),
        scratch_shapes=[pltpu.VMEM((d, tm), F32), pltpu.VMEM((te, tm), BF16)],
        compiler_params=_cp(("parallel", "arbitrary")),
        name="peer_experts",
    )(h, cnt, rfac, rank, e2, u, vt, x, gate)


def _dilated_mult_table(tq, tk, n_off):
    o = jnp.arange(n_off, dtype=jnp.int32)[:, None, None]
    r = jnp.arange(tq, dtype=jnp.int32)[None, :, None]
    c = jnp.arange(tk, dtype=jnp.int32)[None, None, :]
    d = o * tk + r - c
    mult = jnp.zeros((n_off, tq, tk), F32)
    for win, dil in DILATED_PATTERNS:
        steps = win // dil
        ok = (d >= 0) & (d % dil == 0) & (d <= steps * dil)
        mult = mult + ok.astype(F32)
    return mult


def _dil_attn_kernel(q_ref, k_ref, v_ref, mult_ref, o_ref, m_ref, l_ref, acc_ref, *, t, n_off, scale):
    qi = pl.program_id(2)
    q = (q_ref[...].astype(F32) * scale).astype(BF16)
    m_ref[...] = jnp.full_like(m_ref, NEG_BIG)
    l_ref[...] = jnp.zeros_like(l_ref)
    acc_ref[...] = jnp.zeros_like(acc_ref)

    def body(kb, carry):
        off = qi - kb
        start = pl.multiple_of(kb * t, t)
        k = k_ref[pl.ds(start, t), :]
        v = v_ref[pl.ds(start, t), :]
        s = lax.dot_general(q, k, NT_DIMS, preferred_element_type=F32)
        mult = mult_ref[off]
        s = jnp.where(mult > 0.0, s, NEG_BIG)
        m_old = m_ref[...]
        m_new = jnp.maximum(m_old, jnp.max(s, axis=-1, keepdims=True))
        alpha = jnp.exp(m_old - m_new)
        p = jnp.exp(s - jnp.tile(m_new, (1, t // LANE))) * mult
        l_ref[...] = alpha * l_ref[...] + jnp.sum(p, axis=-1, keepdims=True)
        acc_ref[...] = alpha * acc_ref[...] + jnp.dot(p.astype(BF16), v, preferred_element_type=F32)
        m_ref[...] = m_new
        return carry

    lax.fori_loop(jnp.maximum(qi - (n_off - 1), 0), qi + 1, body, 0)
    o_ref[...] = (acc_ref[...] / l_ref[...]).astype(o_ref.dtype)


def dilated_attention_prompt(proj_bf, *, batch, seq, heads, t):
    win_max = min(max(w for w, _ in DILATED_PATTERNS), seq)
    n_off = min(-(-win_max // t) + 1, seq // t)
    mult = _dilated_mult_table(t, t, n_off)
    nq = seq // t
    kern = functools.partial(_dil_attn_kernel, t=t, n_off=n_off, scale=1.0 / math.sqrt(HD_A))
    return pl.pallas_call(
        kern,
        grid=(batch, heads, nq),
        in_specs=[pl.BlockSpec((t, HD_A), lambda b, h, i: (b * nq + i, h)),
                  pl.BlockSpec((seq, HD_A), lambda b, h, i: (b, heads + h)),
                  pl.BlockSpec((seq, HD_A), lambda b, h, i: (b, 2 * heads + h)),
                  pl.BlockSpec((n_off, t, t), lambda b, h, i: (0, 0, 0))],
        out_specs=pl.BlockSpec((t, HD_A), lambda b, h, i: (b * nq + i, h)),
        out_shape=jax.ShapeDtypeStruct((batch * seq, heads * HD_A), BF16),
        scratch_shapes=[pltpu.VMEM((t, LANE), F32), pltpu.VMEM((t, LANE), F32), pltpu.VMEM((t, HD_A), F32)],
        compiler_params=_cp(("parallel", "parallel", "arbitrary")),
        name="dilated_attention",
    )(proj_bf, proj_bf, proj_bf, mult)


def _pair_cols(q, p, lane_lo):
    a = q[:, 2 * p:2 * p + 1]
    b = q[:, 2 * p + 1:2 * p + 2]
    return jnp.where(lane_lo, a, b)


def _ssd_kernel(z_ref, xbc_ref, dt_ref, cw_ref, cb_ref, dtb_ref, alog_ref, dsk_ref, nw_ref,
                y_ref, st_ref, xprev_ref, s_ref, *, d_inner, n_pairs, conv_w):
    c = pl.program_id(1)
    q = xbc_ref.shape[0]
    pre = SUBLANE

    @pl.when(c == 0)
    def _():
        xprev_ref[0:pre, :] = jnp.zeros((pre, xprev_ref.shape[1]), F32)
        s_ref[...] = jnp.zeros_like(s_ref)

    x_in = xbc_ref[...]
    xprev_ref[pre:pre + q, :] = x_in
    conv = cb_ref[...]
    for kk in range(conv_w):
        conv = conv + xprev_ref[pl.ds(pre - (conv_w - 1) + kk, q), :] * cw_ref[kk:kk + 1, :]
    xprev_ref[0:pre, :] = x_in[q - pre:q, :]
    act = _silu(conv)
    gn = SSM_GROUPS * D_STATE
    xs = act[:, :d_inner]
    bm = act[:, d_inner:d_inner + gn]
    cm = act[:, d_inner + gn:d_inner + 2 * gn]

    dt = _softplus(dt_ref[...] + dtb_ref[...])
    a = -jnp.exp(alog_ref[...])
    dta = dt * a
    row = lax.broadcasted_iota(jnp.int32, (q, q), 0)
    col = lax.broadcasted_iota(jnp.int32, (q, q), 1)
    causal = row >= col
    cum = jnp.dot(causal.astype(F32), dta, preferred_element_type=F32, precision=lax.Precision.HIGHEST)
    cum_t = cum.T
    dt_t = dt.T
    cum_end = cum[q - 1:q, :]
    wdec = jnp.exp(cum_end - cum) * dt
    ecum = jnp.exp(cum)
    cdec = jnp.exp(cum_end)
    lane_lo = lax.broadcasted_iota(jnp.int32, (q, LANE), 1) < SSM_HEADDIM
    lane_lo1 = lane_lo[0:1, :]
    pairs_per_group = (n_pairs * 2 // SSM_GROUPS) // 2

    ys = []
    for g in range(SSM_GROUPS):
        bg = bm[:, g * D_STATE:(g + 1) * D_STATE]
        cg = cm[:, g * D_STATE:(g + 1) * D_STATE].astype(BF16)
        cb = lax.dot_general(cg, bg.astype(BF16), NT_DIMS, preferred_element_type=F32)
        bg_t = bg.T.astype(BF16)
        for pp in range(pairs_per_group):
            p = g * pairs_per_group + pp
            xp = xs[:, p * LANE:(p + 1) * LANE]
            xpb = xp.astype(BF16)
            y_intra = None
            for hh in range(2):
                h = 2 * p + hh
                seg = cum[:, h:h + 1] - cum_t[h:h + 1, :]
                dec = jnp.exp(jnp.where(causal, seg, -jnp.inf))
                mm = (cb * dec * dt_t[h:h + 1, :]).astype(BF16)
                yh = jnp.dot(mm, xpb, preferred_element_type=F32)
                y_intra = yh if y_intra is None else jnp.where(lane_lo, y_intra, yh)
            sp = s_ref[p]
            y_inter = jnp.dot(cg, sp.astype(BF16), preferred_element_type=F32) * _pair_cols(ecum, p, lane_lo)
            ys.append(y_intra + y_inter + dsk_ref[:, p * LANE:(p + 1) * LANE] * xp)
            xw = (xp * _pair_cols(wdec, p, lane_lo)).astype(BF16)
            s_ref[p] = (_pair_cols(cdec, p, lane_lo1) * sp
                        + jnp.dot(bg_t, xw, preferred_element_type=F32))
    y = jnp.concatenate(ys, axis=1)
    y = y * _silu(z_ref[...])
    gw = d_inner // SSM_GROUPS
    outs = []
    for g in range(SSM_GROUPS):
        yg = y[:, g * gw:(g + 1) * gw]
        outs.append(yg * lax.rsqrt(jnp.mean(yg * yg, axis=-1, keepdims=True) + EPS))
    y_ref[...] = (jnp.concatenate(outs, axis=1) * nw_ref[...]).astype(y_ref.dtype)

    @pl.when(c == pl.num_programs(1) - 1)
    def _():
        st_ref[0] = s_ref[...]


def ssd_prompt(proj, conv_w, conv_b, dtb_pad, alog_pad, dskip_lane, norm_w, *, batch, seq, d_inner, col0):
    q = SSD_CHUNK
    nc = seq // q
    conv_dim = conv_w.shape[1]
    cw = conv_w.shape[0]
    n_pairs = d_inner // (2 * SSM_HEADDIM)
    zb = col0 // d_inner
    xb = (col0 + d_inner) // conv_dim
    db = (col0 + d_inner + conv_dim) // LANE
    kern = functools.partial(_ssd_kernel, d_inner=d_inner, n_pairs=n_pairs, conv_w=cw)
    const = lambda b, c: (0, 0)
    return pl.pallas_call(
        kern,
        grid=(batch, nc),
        in_specs=[pl.BlockSpec((q, d_inner), lambda b, c: (b * nc + c, zb)),
                  pl.BlockSpec((q, conv_dim), lambda b, c: (b * nc + c, xb)),
                  pl.BlockSpec((q, LANE), lambda b, c: (b * nc + c, db)),
                  pl.BlockSpec((cw, conv_dim), const),
                  pl.BlockSpec((1, conv_dim), const),
                  pl.BlockSpec((1, LANE), const),
                  pl.BlockSpec((1, LANE), const),
                  pl.BlockSpec((1, d_inner), const),
                  pl.BlockSpec((1, d_inner), const)],
        out_specs=[pl.BlockSpec((q, d_inner), lambda b, c: (b * nc + c, 0)),
                   pl.BlockSpec((1, n_pairs, D_STATE, LANE), lambda b, c: (b, 0, 0, 0))],
        out_shape=[jax.ShapeDtypeStruct((batch * seq, d_inner), BF16),
                   jax.ShapeDtypeStruct((batch, n_pairs, D_STATE, LANE), F32)],
        scratch_shapes=[pltpu.VMEM((SUBLANE + q, conv_dim), F32),
                        pltpu.VMEM((n_pairs, D_STATE, LANE), F32)],
        compiler_params=_cp(("parallel", "arbitrary")),
        name="ssd_prompt",
    )(proj, proj, proj, conv_w, conv_b, dtb_pad, alog_pad, dskip_lane, norm_w)


def _state_from_pairs(st):
    b, npair, n, _ = st.shape
    return st.reshape(b, npair, n, 2, SSM_HEADDIM).transpose(0, 1, 3, 4, 2).reshape(b, 2 * npair, SSM_HEADDIM, n)


def _diff_lambda(lamp, lam_init):
    l1 = jnp.sum(lamp[0:1] * lamp[1:2], axis=-1, keepdims=True)
    l2 = jnp.sum(lamp[2:3] * lamp[3:4], axis=-1, keepdims=True)
    return jnp.exp(l1) - jnp.exp(l2) + lam_init


def _diff_finish(acc0, l0, acc1, l1, lam, sw, lam_init):
    o = acc0 / l0 - lam * (acc1 / l1)
    o = o * lax.rsqrt(jnp.mean(o * o, axis=-1, keepdims=True) + EPS) * sw
    return o * (1.0 - lam_init)


def _diff_attn_kernel(lamp_ref, sw_ref, q_ref, k_ref, v_ref, o_ref, m_ref, l_ref, acc_ref, *, t, scale, lam_init):
    qi = pl.program_id(2)
    q = (q_ref[...].astype(F32) * scale).astype(BF16)
    m_ref[...] = jnp.full_like(m_ref, NEG_BIG)
    l_ref[...] = jnp.zeros_like(l_ref)
    acc_ref[...] = jnp.zeros_like(acc_ref)
    row = lax.broadcasted_iota(jnp.int32, (t, t), 0)
    col = lax.broadcasted_iota(jnp.int32, (t, t), 1)

    def step(kb, masked):
        start = pl.multiple_of(kb * t, t)
        k = k_ref[pl.ds(start, t), :]
        v = v_ref[pl.ds(start, t), :]
        for m in range(2):
            s = lax.dot_general(q[:, m * HD_C:(m + 1) * HD_C], k[:, m * HD_C:(m + 1) * HD_C], NT_DIMS,
                                preferred_element_type=F32)
            if masked:
                s = jnp.where(row >= col, s, NEG_BIG)
            m_old = m_ref[m]
            m_new = jnp.maximum(m_old, jnp.max(s, axis=-1, keepdims=True))
            alpha = jnp.exp(m_old - m_new)
            p = jnp.exp(s - jnp.tile(m_new, (1, t // LANE)))
            l_ref[m] = alpha * l_ref[m] + jnp.sum(p, axis=-1, keepdims=True)
            acc_ref[m] = (jnp.tile(alpha, (1, v.shape[1] // LANE)) * acc_ref[m]
                          + jnp.dot(p.astype(BF16), v, preferred_element_type=F32))
            m_ref[m] = m_new

    def body(kb, carry):
        step(kb, False)
        return carry

    lax.fori_loop(0, qi, body, 0)
    step(qi, True)
    lam = _diff_lambda(lamp_ref[...], lam_init)
    o_ref[...] = _diff_finish(acc_ref[0], l_ref[0][:, 0:1], acc_ref[1], l_ref[1][:, 0:1], lam, sw_ref[...],
                              lam_init).astype(o_ref.dtype)


def diff_attention_prompt(proj_bf, lamp, subln_w, *, batch, seq, heads, t, lam_init):
    hw = 2 * HD_C
    nq = seq // t
    kern = functools.partial(_diff_attn_kernel, t=t, scale=1.0 / math.sqrt(HD_C), lam_init=lam_init)
    return pl.pallas_call(
        kern,
        grid=(batch, heads, nq),
        in_specs=[pl.BlockSpec((4, HD_C), lambda b, h, i: (0, 0)),
                  pl.BlockSpec((1, hw), lambda b, h, i: (0, 0)),
                  pl.BlockSpec((t, hw), lambda b, h, i: (b * nq + i, h)),
                  pl.BlockSpec((seq, hw), lambda b, h, i: (b, heads + h)),
                  pl.BlockSpec((seq, hw), lambda b, h, i: (b, 2 * heads + h))],
        out_specs=pl.BlockSpec((t, hw), lambda b, h, i: (b * nq + i, h)),
        out_shape=jax.ShapeDtypeStruct((batch * seq, heads * hw), BF16),
        scratch_shapes=[pltpu.VMEM((2, t, LANE), F32), pltpu.VMEM((2, t, LANE), F32), pltpu.VMEM((2, t, hw), F32)],
        compiler_params=_cp(("parallel", "parallel", "arbitrary")),
        name="diff_attention",
    )(lamp, subln_w, proj_bf, proj_bf, proj_bf)


def _diff_sample_kernel(pt_ref, q_ref, kn_ref, vn_ref, lamp_ref, sw_ref, *rest, pp, scale, lam_init):
    del pt_ref
    k_refs = rest[:pp]
    v_refs = rest[pp:2 * pp]
    o_ref, m_ref, l_ref, acc_ref = rest[2 * pp:]
    pg = pl.program_id(1)
    q = q_ref[0] * scale

    @pl.when(pg == 0)
    def _():
        prod = q * kn_ref[0]
        for m in range(2):
            m_ref[m] = jnp.sum(prod[:, m * HD_C:(m + 1) * HD_C], axis=-1, keepdims=True)
            l_ref[m] = jnp.ones_like(l_ref[m])
            acc_ref[m] = vn_ref[0]

    for i in range(pp):
        kk = k_refs[i][0, 0]
        vv = v_refs[i][0, 0]
        prod = kk * q[None]
        for m in range(2):
            s = jnp.sum(prod[:, :, m * HD_C:(m + 1) * HD_C], axis=-1, keepdims=True)
            m_old = m_ref[m]
            m_new = jnp.maximum(m_old, jnp.max(s, axis=0))
            alpha = jnp.exp(m_old - m_new)
            p = jnp.exp(s - m_new[None])
            l_ref[m] = alpha * l_ref[m] + jnp.sum(p, axis=0)
            acc_ref[m] = alpha * acc_ref[m] + jnp.sum(p * vv, axis=0)
            m_ref[m] = m_new

    @pl.when(pg == pl.num_programs(1) - 1)
    def _():
        lam = _diff_lambda(lamp_ref[...], lam_init)
        o_ref[0] = _diff_finish(acc_ref[0], l_ref[0], acc_ref[1], l_ref[1], lam, sw_ref[...], lam_init)


def diff_attention_sample(page_table, q3, kn3, vn3, lamp, subln_w, cache_k, cache_v, *, layer, pp, lam_init):
    bsz, heads, hw = q3.shape
    n_pages = page_table.shape[1]
    page = cache_k.shape[2]
    row_spec = pl.BlockSpec((1, heads, hw), lambda b, p, pt: (b, 0, 0))

    def page_spec(i):
        return pl.BlockSpec((1, 1, page, heads, hw), lambda b, p, pt: (layer, pt[b, p * pp + i], 0, 0, 0))

    kern = functools.partial(_diff_sample_kernel, pp=pp, scale=1.0 / math.sqrt(HD_C), lam_init=lam_init)
    grid_spec = pltpu.PrefetchScalarGridSpec(
        num_scalar_prefetch=1,
        grid=(bsz, n_pages // pp),
        in_specs=[row_spec, row_spec, row_spec,
                  pl.BlockSpec((4, HD_C), lambda b, p, pt: (0, 0)),
                  pl.BlockSpec((1, hw), lambda b, p, pt: (0, 0))]
                 + [page_spec(i) for i in range(pp)] + [page_spec(i) for i in range(pp)],
        out_specs=row_spec,
        scratch_shapes=[pltpu.VMEM((2, heads, 1), F32), pltpu.VMEM((2, heads, 1), F32),
                        pltpu.VMEM((2, heads, hw), F32)],
    )
    return pl.pallas_call(
        kern,
        grid_spec=grid_spec,
        out_shape=jax.ShapeDtypeStruct((bsz, heads, hw), F32),
        compiler_params=_cp(("parallel", "arbitrary")),
        name="diff_attention_sample",
    )(page_table, q3, kn3, vn3, lamp, subln_w, *([cache_k] * pp), *([cache_v] * pp))


def _win_sample_kernel(q_ref, kn_ref, vn_ref, ck_ref, cv_ref, o_ref, m_ref, l_ref, acc_ref, *, tr, wb, scale):
    r = pl.program_id(1)
    q = q_ref[0] * scale

    @pl.when(r == 0)
    def _():
        m_ref[...] = jnp.sum(q * kn_ref[0], axis=-1, keepdims=True)
        l_ref[...] = jnp.full_like(l_ref, float(len(DILATED_PATTERNS)))
        acc_ref[...] = float(len(DILATED_PATTERNS)) * vn_ref[0]

    kk = ck_ref[0, 0]
    s = jnp.sum(kk * q[None], axis=-1, keepdims=True)
    j = r * tr + lax.broadcasted_iota(jnp.int32, s.shape, 0)
    d = wb - j
    mult = jnp.zeros(s.shape, F32)
    for win, dil in DILATED_PATTERNS:
        steps = win // dil
        mult = mult + ((d % dil == 0) & (d <= steps * dil)).astype(F32)
    s = jnp.where(mult > 0.0, s, NEG_BIG)
    m_old = m_ref[...]
    m_new = jnp.maximum(m_old, jnp.max(s, axis=0))
    alpha = jnp.exp(m_old - m_new)
    p = jnp.exp(s - m_new[None]) * mult
    l_ref[...] = alpha * l_ref[...] + jnp.sum(p, axis=0)
    acc_ref[...] = alpha * acc_ref[...] + jnp.sum(p * cv_ref[0, 0], axis=0)
    m_ref[...] = m_new

    @pl.when(r == pl.num_programs(1) - 1)
    def _():
        o_ref[0] = acc_ref[...] / l_ref[...]


def dilated_attention_sample(q3, kn3, vn3, cache_k, cache_v, layer, *, tr):
    bsz, heads, hd = q3.shape
    wb = cache_k.shape[2]
    row_spec = pl.BlockSpec((1, heads, hd), lambda b, r: (b, 0, 0))
    c_spec = pl.BlockSpec((1, 1, tr, heads, hd), lambda b, r: (layer, b, r, 0, 0))
    kern = functools.partial(_win_sample_kernel, tr=tr, wb=wb, scale=1.0 / math.sqrt(hd))
    return pl.pallas_call(
        kern,
        grid=(bsz, wb // tr),
        in_specs=[row_spec, row_spec, row_spec, c_spec, c_spec],
        out_specs=row_spec,
        out_shape=jax.ShapeDtypeStruct((bsz, heads, hd), F32),
        scratch_shapes=[pltpu.VMEM((heads, 1), F32), pltpu.VMEM((heads, 1), F32), pltpu.VMEM((heads, hd), F32)],
        compiler_params=_cp(("parallel", "arbitrary")),
        name="dilated_attention_sample",
    )(q3, kn3, vn3, cache_k, cache_v)


def _lane_to_col(v):
    n = v.shape[1] // LANE
    return jnp.concatenate([jnp.broadcast_to(v[:, k * LANE:(k + 1) * LANE], (LANE, LANE)).T for k in range(n)], axis=0)


def _col_to_lane(c):
    n = c.shape[0] // LANE
    return jnp.concatenate([jnp.broadcast_to(c[k * LANE:(k + 1) * LANE], (LANE, LANE)).T[0:1] for k in range(n)], axis=1)


def _ssd_step_kernel(z_ref, xbc_ref, dt_ref, cs_ref, st_ref, cw_ref, cb_ref, dtb_ref, alog_ref, dsk_ref, nw_ref,
                     e_ref, y_ref, cso_ref, sto_ref, *, d_inner):
    x = xbc_ref[0]
    cs = cs_ref[0]
    ncs = cs.shape[0]
    conv = cb_ref[...] + x * cw_ref[ncs:ncs + 1, :]
    for kk in range(ncs):
        conv = conv + cs[kk:kk + 1, :] * cw_ref[kk:kk + 1, :]
    cso_ref[0, 0:ncs - 1, :] = cs[1:ncs, :]
    cso_ref[0, ncs - 1:ncs, :] = x
    act = _silu(conv)
    gn = SSM_GROUPS * D_STATE
    xs = act[:, :d_inner]
    bm = act[:, d_inner:d_inner + gn]
    cm = act[:, d_inner + gn:d_inner + 2 * gn]
    dt = _softplus(dt_ref[0] + dtb_ref[...])
    dt_lane = jnp.dot(jnp.broadcast_to(dt, (SUBLANE, LANE)), e_ref[...], preferred_element_type=F32,
                      precision=lax.Precision.HIGHEST)[0:1]
    dec_lane = jnp.exp(dt_lane * (-jnp.exp(alog_ref[...])))
    dec_col = _lane_to_col(dec_lane)
    dtx_col = _lane_to_col(dt_lane * xs)
    rows_g = d_inner // SSM_GROUPS
    b_rows = jnp.concatenate([jnp.broadcast_to(bm[:, g * D_STATE:(g + 1) * D_STATE], (rows_g, D_STATE))
                              for g in range(SSM_GROUPS)], axis=0)
    c_rows = jnp.concatenate([jnp.broadcast_to(cm[:, g * D_STATE:(g + 1) * D_STATE], (rows_g, D_STATE))
                              for g in range(SSM_GROUPS)], axis=0)
    st = st_ref[0].reshape(d_inner, D_STATE)
    st_new = dec_col * st + dtx_col * b_rows
    sto_ref[0] = st_new.reshape(sto_ref.shape[1:])
    y = _col_to_lane(jnp.sum(st_new * c_rows, axis=-1, keepdims=True))
    y = y + dsk_ref[...] * xs
    y = y * _silu(z_ref[0])
    gw = d_inner // SSM_GROUPS
    outs = []
    for g in range(SSM_GROUPS):
        yg = y[:, g * gw:(g + 1) * gw]
        outs.append(yg * lax.rsqrt(jnp.mean(yg * yg, axis=-1, keepdims=True) + EPS))
    y_ref[0] = jnp.concatenate(outs, axis=1) * nw_ref[...]


def ssd_step(z3, xbc3, dt3, conv_state, ssm_state, conv_w, conv_b, dtb_pad, alog_lane, dskip_lane, norm_w, expand):
    bsz, _, d_inner = z3.shape
    conv_dim = xbc3.shape[2]
    ncs = conv_state.shape[1]
    hb, hp, n = ssm_state.shape[1:]
    const = lambda b: (0, 0)
    kern = functools.partial(_ssd_step_kernel, d_inner=d_inner)
    return pl.pallas_call(
        kern,
        grid=(bsz,),
        in_specs=[pl.BlockSpec((1, 1, d_inner), lambda b: (b, 0, 0)),
                  pl.BlockSpec((1, 1, conv_dim), lambda b: (b, 0, 0)),
                  pl.BlockSpec((1, 1, LANE), lambda b: (b, 0, 0)),
                  pl.BlockSpec((1, ncs, conv_dim), lambda b: (b, 0, 0)),
                  pl.BlockSpec((1, hb, hp, n), lambda b: (b, 0, 0, 0)),
                  pl.BlockSpec((ncs + 1, conv_dim), const),
                  pl.BlockSpec((1, conv_dim), const),
                  pl.BlockSpec((1, LANE), const),
                  pl.BlockSpec((1, d_inner), const),
                  pl.BlockSpec((1, d_inner), const),
                  pl.BlockSpec((1, d_inner), const),
                  pl.BlockSpec((LANE, d_inner), const)],
        out_specs=[pl.BlockSpec((1, 1, d_inner), lambda b: (b, 0, 0)),
                   pl.BlockSpec((1, ncs, conv_dim), lambda b: (b, 0, 0)),
                   pl.BlockSpec((1, hb, hp, n), lambda b: (b, 0, 0, 0))],
        out_shape=[jax.ShapeDtypeStruct((bsz, 1, d_inner), F32),
                   jax.ShapeDtypeStruct(conv_state.shape, F32),
                   jax.ShapeDtypeStruct(ssm_state.shape, F32)],
        compiler_params=_cp(("parallel",)),
        name="ssd_step",
    )(z3, xbc3, dt3, conv_state, ssm_state, conv_w, conv_b, dtb_pad, alog_lane, dskip_lane, norm_w, expand)


PROMPT_TM = 512
NORMMOD_TM = 1024
ROUTE_TM = 256
EXPERT_NI = 8
EXPERT_SUB = 2
ATTN_T = 512
SAMPLE_PAD = LANE
WIN_SAMPLE_ROWS = 512
PAGES_PER_STEP = 8


def _pad_cols(w, n):
    return jnp.pad(w, ((0, 0), (0, n - w.shape[1])))


def _pad_rows(a, n):
    return jnp.pad(a, ((0, n - a.shape[0]),) + ((0, 0),) * (a.ndim - 1))


def _round_up(n, m):
    return -(-n // m) * m


def kernel(x_prompt, x_sample, c_prompt, c_sample, cache_win_k, cache_win_v, state_conv, state_ssm,
           cache_diff_k, cache_diff_v, page_table, norm1_w, norm2_w, w_ada, b_ada, w_in_even, w_out_even,
           conv_w, conv_b, dt_bias, a_log, d_skip, ssm_norm_w, w_in_odd, w_out_odd, lambda_q1, lambda_k1,
           lambda_q2, lambda_k2, subln_w, peer_wq, peer_keys, peer_u, peer_v, final_norm_w):
    bsz, seq, d = x_prompt.shape
    sb, st, _ = x_sample.shape
    assert st == 1, "one new token per sample sequence"
    depth = w_ada.shape[0]
    h_a = cache_win_k.shape[3]
    a_width = h_a * HD_A
    d_inner = ssm_norm_w.shape[1]
    conv_dim = conv_w.shape[2]
    h_b = dt_bias.shape[1]
    h_c = cache_diff_k.shape[3]
    c_width = h_c * 2 * HD_C
    wb_p = min(max(w for w, _ in DILATED_PATTERNS), seq)
    ncs = conv_w.shape[1] - 1

    xp = x_prompt.reshape(bsz * seq, d)
    xs = x_sample.reshape(sb, d)

    n_c = _round_up(bsz + sb, SUBLANE)
    c_all = _pad_rows(jnp.concatenate([c_prompt, c_sample], axis=0), n_c)
    mod = adaln_all(c_all, w_ada, b_ada)

    expand = (jnp.arange(LANE)[:, None] == (jnp.arange(d_inner) // SSM_HEADDIM)[None, :]).astype(F32)

    wq_all = _split_bf16(jnp.swapaxes(peer_wq, 1, 2))
    nhc = peer_keys.shape[1] * peer_keys.shape[2]
    keys_all = _split_bf16(peer_keys.reshape(depth, nhc, peer_keys.shape[3], peer_keys.shape[4]))
    u_all = peer_u.astype(BF16)
    vt_all = jnp.swapaxes(peer_v, 1, 2).astype(BF16)

    outs = {k: [] for k in ("pwk", "pwv", "pcv", "pss", "pdk", "pdv", "swk", "swv", "scv", "sss", "sdk", "sdv")}
    for l in range(depth):
        parts = [mod[l, :, i * d:(i + 1) * d] for i in range(6)]
        pm = [p[:bsz][:, None, :] for p in parts]
        sm = [p[bsz:bsz + sb][None] for p in parts]
        n1 = norm1_w[l][None]
        n2 = norm2_w[l][None]
        if l % 2 == 0:
            e = l // 2
            even_in = w_in_even.shape[2]
            n_pad = _round_up(even_in, 7 * LANE)
            w_in = _pad_cols(w_in_even[e], n_pad).astype(BF16)
            tn = n_pad // 7
            wo = w_out_even[e].astype(BF16)
            dtb_pad = _pad_cols(dt_bias[e][None], LANE)
            alog_pad = _pad_cols(a_log[e][None], LANE)
            alog_lane = jnp.repeat(a_log[e], SSM_HEADDIM)[None]
            dsk_lane = jnp.repeat(d_skip[e], SSM_HEADDIM)[None]
            cw, cb, nw = conv_w[e], conv_b[e][None], ssm_norm_w[e][None]
            col_z = 3 * a_width

            proj, proj_bf = normmod_matmul(xp, n1, pm[0], pm[1], w_in, tm=min(NORMMOD_TM, seq), tn=tn,
                                           rows_per_group=seq)
            ya = dilated_attention_prompt(proj_bf, batch=bsz, seq=seq, heads=h_a, t=ATTN_T)
            yb, st_pairs = ssd_prompt(proj, cw, cb, dtb_pad, alog_pad, dsk_lane, nw,
                                      batch=bsz, seq=seq, d_inner=d_inner, col0=col_z)
            xp = matmul_residual([ya, yb], [wo[:a_width], wo[a_width:]], xp, pm[2],
                                 tm=PROMPT_TM, tn=1024, rows_per_group=seq)
            proj3 = proj.reshape(bsz, seq, n_pad)
            outs["pwk"].append(proj3[:, seq - wb_p:, a_width:2 * a_width].reshape(bsz, wb_p, h_a, HD_A))
            outs["pwv"].append(proj3[:, seq - wb_p:, 2 * a_width:3 * a_width].reshape(bsz, wb_p, h_a, HD_A))
            outs["pcv"].append(proj3[:, seq - ncs:, col_z + d_inner:col_z + d_inner + conv_dim])
            outs["pss"].append(_state_from_pairs(st_pairs))

            sproj, _ = normmod_matmul(xs, n1, sm[0], sm[1], w_in, tm=sb, tn=tn, rows_per_group=sb)
            q3 = sproj[:, :a_width].reshape(sb, h_a, HD_A)
            k3 = sproj[:, a_width:2 * a_width].reshape(sb, h_a, HD_A)
            v3 = sproj[:, 2 * a_width:3 * a_width].reshape(sb, h_a, HD_A)
            ya_s = dilated_attention_sample(q3, k3, v3, cache_win_k, cache_win_v, e,
                                            tr=min(WIN_SAMPLE_ROWS, cache_win_k.shape[2]))
            z3 = sproj[:, col_z:col_z + d_inner][:, None, :]
            xbc3 = sproj[:, col_z + d_inner:col_z + d_inner + conv_dim][:, None, :]
            dt3 = sproj[:, col_z + d_inner + conv_dim:col_z + d_inner + conv_dim + LANE][:, None, :]
            yb_s, cs_new, ss_new = ssd_step(z3, xbc3, dt3, state_conv[e], state_ssm[e], cw, cb, dtb_pad,
                                            alog_lane, dsk_lane, nw, expand)
            xs = matmul_residual([ya_s.reshape(sb, a_width).astype(BF16), yb_s.reshape(sb, d_inner).astype(BF16)],
                                 [wo[:a_width], wo[a_width:]], xs, sm[2], tm=sb, tn=1024, rows_per_group=sb)
            outs["swk"].append(k3[:, None])
            outs["swv"].append(v3[:, None])
            outs["scv"].append(cs_new)
            outs["sss"].append(ss_new)
        else:
            o_ = l // 2
            lam_init = 0.8 - 0.6 * math.exp(-0.3 * l)
            w_in = w_in_odd[o_].astype(BF16)
            wo = w_out_odd[o_].astype(BF16)
            lamp = jnp.stack([lambda_q1[o_], lambda_k1[o_], lambda_q2[o_], lambda_k2[o_]])
            sw = subln_w[o_][None]

            proj, proj_bf = normmod_matmul(xp, n1, pm[0], pm[1], w_in, tm=min(NORMMOD_TM, seq), tn=1024,
                                           rows_per_group=seq)
            o = diff_attention_prompt(proj_bf, lamp, sw, batch=bsz, seq=seq, heads=h_c, t=ATTN_T, lam_init=lam_init)
            xp = matmul_residual([o], [wo], xp, pm[2], tm=PROMPT_TM, tn=1024, rows_per_group=seq)
            outs["pdk"].append(proj[:, c_width:2 * c_width].reshape(bsz, seq, h_c, 2 * HD_C))
            outs["pdv"].append(proj[:, 2 * c_width:3 * c_width].reshape(bsz, seq, h_c, 2 * HD_C))

            sproj, _ = normmod_matmul(xs, n1, sm[0], sm[1], w_in, tm=sb, tn=1024, rows_per_group=sb)
            q3 = sproj[:, :c_width].reshape(sb, h_c, 2 * HD_C)
            k3 = sproj[:, c_width:2 * c_width].reshape(sb, h_c, 2 * HD_C)
            v3 = sproj[:, 2 * c_width:3 * c_width].reshape(sb, h_c, 2 * HD_C)
            o_s = diff_attention_sample(page_table, q3, k3, v3, lamp, sw, cache_diff_k, cache_diff_v,
                                        layer=o_, pp=PAGES_PER_STEP, lam_init=lam_init)
            xs = matmul_residual([o_s.reshape(sb, c_width).astype(BF16)], [wo], xs, sm[2],
                                 tm=sb, tn=1024, rows_per_group=sb)
            outs["sdk"].append(k3[:, None])
            outs["sdv"].append(v3[:, None])

        routed = peer_route(xp, n2, pm[3], pm[4], wq_all, keys_all, l, tm=ROUTE_TM, rows_per_group=seq)
        xp = peer_experts(*routed, u_all, vt_all, l, xp, pm[5], tm=PROMPT_TM, ni=EXPERT_NI, sub=EXPERT_SUB,
                          rows_per_group=seq)

        xs_pad = _pad_rows(xs, SAMPLE_PAD)
        smp = [jnp.pad(m_, ((0, 0), (0, SAMPLE_PAD - sb), (0, 0))) for m_ in sm[3:6]]
        routed = peer_route(xs_pad, n2, smp[0], smp[1], wq_all, keys_all, l, tm=SAMPLE_PAD, rows_per_group=SAMPLE_PAD)
        xs = peer_experts(*routed, u_all, vt_all, l, xs_pad, smp[2], tm=SAMPLE_PAD, ni=EXPERT_NI, sub=EXPERT_SUB,
                          rows_per_group=SAMPLE_PAD)[:sb]

    y_prompt = rmsnorm_rows(xp, final_norm_w[None], tm=PROMPT_TM).reshape(bsz, seq, d)
    y_sample = rmsnorm_rows(xs, final_norm_w[None], tm=sb).reshape(sb, st, d)
    stk = lambda k: jnp.stack(outs[k])
    return (y_prompt, y_sample, stk("pwk"), stk("pwv"), stk("pcv"), stk("pss"), stk("pdk"), stk("pdv"),
            stk("swk"), stk("swv"), stk("scv"), stk("sss"), stk("sdk"), stk("sdv"))
```

```python
import functools
import math

import jax
import jax.numpy as jnp
from jax import lax
from jax.experimental import pallas as pl
from jax.experimental.pallas import tpu as pltpu

F32 = jnp.float32
BF16 = jnp.bfloat16
EPS = 1e-6
NEG_BIG = -1e30

HD_A = 128
HD_C = 128
DILATED_PATTERNS = ((128, 1), (512, 4), (2048, 16))
SSM_HEADDIM = 64
SSM_GROUPS = 4
D_STATE = 128
SSD_CHUNK = 128
PEER_HEADS = 8
PEER_TOPK = 16

V7X_VMEM_LIMIT = 58 * 1024 * 1024
LANE = 128
SUBLANE = 8
BF16_ROWS = 16
WEIGHT_ROW_BLOCKS = 4

NT_DIMS = (((1,), (1,)), ((), ()))


def _cp(sem, vmem=V7X_VMEM_LIMIT):
    return pltpu.CompilerParams(dimension_semantics=sem, vmem_limit_bytes=vmem)


def _sigmoid(x):
    return 1.0 / (1.0 + jnp.exp(-x))


def _silu(x):
    return x * _sigmoid(x)


def _softplus(x):
    return jnp.maximum(x, 0.0) + jnp.log1p(jnp.exp(-jnp.abs(x)))


def _norm_modulate(x, nw, sh, sc):
    ms = jnp.mean(x * x, axis=-1, keepdims=True)
    y = x * lax.rsqrt(ms + EPS) * nw
    return y * (1.0 + sc) + sh


def _adaln_kernel(c_ref, w_ref, b_ref, o_ref):
    a_hi, a_lo = _split_bf16(_silu(c_ref[...]))
    w_hi, w_lo = _split_bf16(w_ref[0])
    o_ref[0] = _dot3(a_hi, a_lo, w_hi, w_lo, NN_DIMS) + b_ref[0]


def adaln_all(c_all, w_ada, b_ada):
    nl, d, n = w_ada.shape
    r = c_all.shape[0]
    tn = 1024
    return pl.pallas_call(
        _adaln_kernel,
        grid=(nl, n // tn),
        in_specs=[pl.BlockSpec((r, d), lambda l, j: (0, 0)),
                  pl.BlockSpec((1, d, tn), lambda l, j: (l, 0, j)),
                  pl.BlockSpec((1, 1, tn), lambda l, j: (l, 0, j))],
        out_specs=pl.BlockSpec((1, r, tn), lambda l, j: (l, 0, j)),
        out_shape=jax.ShapeDtypeStruct((nl, r, n), F32),
        compiler_params=_cp(("parallel", "parallel")),
        name="adaln",
    )(c_all, w_ada, b_ada.reshape(nl, 1, n))


def _normmod_mm_kernel(x_ref, nw_ref, sh_ref, sc_ref, *rest, ks):
    w_refs = rest[:ks]
    o_ref, ob_ref, h_ref = rest[ks:]

    @pl.when(pl.program_id(1) == 0)
    def _():
        h_ref[...] = _norm_modulate(x_ref[...], nw_ref[...], sh_ref[0], sc_ref[0]).astype(BF16)

    kc = h_ref.shape[1] // ks
    o = jnp.dot(h_ref[:, 0:kc], w_refs[0][...], preferred_element_type=F32)
    for c in range(1, ks):
        o = o + jnp.dot(h_ref[:, c * kc:(c + 1) * kc], w_refs[c][...], preferred_element_type=F32)
    o_ref[...] = o
    ob_ref[...] = o.astype(BF16)


def normmod_matmul(x, nw, sh, sc, w, *, tm, tn, rows_per_group):
    t, d = x.shape
    n = w.shape[1]
    r = sh.shape[1]
    tpg = rows_per_group // tm
    mod_spec = pl.BlockSpec((1, r, d), lambda i, j: (i // tpg, 0, 0))
    ks = WEIGHT_ROW_BLOCKS
    return pl.pallas_call(
        functools.partial(_normmod_mm_kernel, ks=ks),
        grid=(t // tm, n // tn),
        in_specs=[pl.BlockSpec((tm, d), lambda i, j: (i, 0)),
                  pl.BlockSpec((1, d), lambda i, j: (0, 0)),
                  mod_spec, mod_spec,
                  *[pl.BlockSpec((d // ks, tn), lambda i, j, c=c: (c, j)) for c in range(ks)]],
        out_specs=[pl.BlockSpec((tm, tn), lambda i, j: (i, j)),
                   pl.BlockSpec((tm, tn), lambda i, j: (i, j))],
        out_shape=[jax.ShapeDtypeStruct((t, n), F32), jax.ShapeDtypeStruct((t, n), BF16)],
        scratch_shapes=[pltpu.VMEM((tm, d), BF16)],
        compiler_params=_cp(("parallel", "arbitrary")),
        name="normmod_matmul",
    )(x, nw, sh, sc, *([w] * ks))


def _normmod_mm_kv_kernel(x_ref, nw_ref, sh_ref, sc_ref, *rest, ks, nseg):
    w_refs = rest[:ks]
    _, _, ob_ref, k_ref, v_ref, h_ref = rest[ks:]
    j = pl.program_id(1)

    @pl.when(j == 0)
    def _():
        h_ref[...] = _norm_modulate(x_ref[...], nw_ref[...], sh_ref[0], sc_ref[0]).astype(BF16)

    kc = h_ref.shape[1] // ks
    o = jnp.dot(h_ref[:, 0:kc], w_refs[0][...], preferred_element_type=F32)
    for c in range(1, ks):
        o = o + jnp.dot(h_ref[:, c * kc:(c + 1) * kc], w_refs[c][...], preferred_element_type=F32)
    ob_ref[...] = o.astype(BF16)

    @pl.when((j >= nseg) & (j < 2 * nseg))
    def _():
        k_ref[0] = o

    @pl.when(j >= 2 * nseg)
    def _():
        v_ref[0] = o


def normmod_matmul_kv(x, nw, sh, sc, w, kbuf, vbuf, layer, *, tm, tn, rows_per_group):
    t, d = x.shape
    n = w.shape[1]
    r = sh.shape[1]
    tpg = rows_per_group // tm
    nseg = n // 3 // tn
    mod_spec = pl.BlockSpec((1, r, d), lambda i, j: (i // tpg, 0, 0))
    ks = WEIGHT_ROW_BLOCKS
    any_spec = pl.BlockSpec(memory_space=pl.ANY)
    k_spec = pl.BlockSpec((1, tm, tn), lambda i, j: (layer, i, jnp.clip(j - nseg, 0, nseg - 1)))
    v_spec = pl.BlockSpec((1, tm, tn), lambda i, j: (layer, i, jnp.clip(j - 2 * nseg, 0, nseg - 1)))
    return pl.pallas_call(
        functools.partial(_normmod_mm_kv_kernel, ks=ks, nseg=nseg),
        grid=(t // tm, n // tn),
        in_specs=[pl.BlockSpec((tm, d), lambda i, j: (i, 0)),
                  pl.BlockSpec((1, d), lambda i, j: (0, 0)),
                  mod_spec, mod_spec,
                  *[pl.BlockSpec((d // ks, tn), lambda i, j, c=c: (c, j)) for c in range(ks)],
                  any_spec, any_spec],
        out_specs=[pl.BlockSpec((tm, tn), lambda i, j: (i, j)), k_spec, v_spec],
        out_shape=[jax.ShapeDtypeStruct((t, n), BF16),
                   jax.ShapeDtypeStruct(kbuf.shape, F32), jax.ShapeDtypeStruct(vbuf.shape, F32)],
        input_output_aliases={4 + ks: 1, 5 + ks: 2},
        scratch_shapes=[pltpu.VMEM((tm, d), BF16)],
        compiler_params=_cp(("parallel", "arbitrary")),
        name="normmod_matmul_kv",
    )(x, nw, sh, sc, *([w] * ks), kbuf, vbuf)


def _mm_res_kernel(*refs, n_pairs):
    a_refs = refs[:n_pairs]
    w_refs = refs[n_pairs:2 * n_pairs]
    x_ref, g_ref, o_ref = refs[2 * n_pairs:]
    acc = jnp.dot(a_refs[0][...], w_refs[0][...], preferred_element_type=F32)
    for a_ref, w_ref in zip(a_refs[1:], w_refs[1:]):
        acc = acc + jnp.dot(a_ref[...], w_ref[...], preferred_element_type=F32)
    o_ref[...] = x_ref[...] + g_ref[0] * acc


def matmul_residual(a_list, w_list, x, gate, *, tm, tn, rows_per_group):
    t, n = x.shape
    r = gate.shape[1]
    tpg = rows_per_group // tm
    npairs = len(a_list)
    in_specs = ([pl.BlockSpec((tm, a.shape[1]), lambda i, j: (i, 0)) for a in a_list]
                + [pl.BlockSpec((w.shape[0], tn), lambda i, j: (0, j)) for w in w_list]
                + [pl.BlockSpec((tm, tn), lambda i, j: (i, j)),
                   pl.BlockSpec((1, r, tn), lambda i, j: (i // tpg, 0, j))])
    return pl.pallas_call(
        functools.partial(_mm_res_kernel, n_pairs=npairs),
        grid=(t // tm, n // tn),
        in_specs=in_specs,
        out_specs=pl.BlockSpec((tm, tn), lambda i, j: (i, j)),
        out_shape=jax.ShapeDtypeStruct((t, n), F32),
        compiler_params=_cp(("parallel", "parallel")),
        name="matmul_residual",
    )(*a_list, *w_list, x, gate)


def _rms_kernel(x_ref, w_ref, o_ref):
    x = x_ref[...]
    ms = jnp.mean(x * x, axis=-1, keepdims=True)
    o_ref[...] = x * lax.rsqrt(ms + EPS) * w_ref[...]


def rmsnorm_rows(x, w, *, tm):
    t, d = x.shape
    return pl.pallas_call(
        _rms_kernel,
        grid=(t // tm,),
        in_specs=[pl.BlockSpec((tm, d), lambda i: (i, 0)), pl.BlockSpec((1, d), lambda i: (0, 0))],
        out_specs=pl.BlockSpec((tm, d), lambda i: (i, 0)),
        out_shape=jax.ShapeDtypeStruct((t, d), F32),
        compiler_params=_cp(("parallel",)),
        name="final_rmsnorm",
    )(x, w)


def _top_values_ranked(s, k, want_rank):
    vals = []
    rank = jnp.full(s.shape, float(k), F32) if want_rank else None
    for it in range(k):
        m = jnp.max(s, axis=0, keepdims=True)
        hit = s == m
        if want_rank:
            rank = jnp.where(hit, float(it), rank)
        vals.append(m)
        if it + 1 < k:
            s = jnp.where(hit, -jnp.inf, s)
    return vals, rank


def _split_bf16(a):
    hi = a.astype(BF16)
    lo = (a - hi.astype(F32)).astype(BF16)
    return hi, lo


def _dot3(a_hi, a_lo, b_hi, b_lo, dims):
    f = functools.partial(lax.dot_general, dimension_numbers=dims, preferred_element_type=F32)
    return f(a_hi, b_hi) + (f(a_hi, b_lo) + f(a_lo, b_hi))


NN_DIMS = (((1,), (0,)), ((), ()))


def _peer_route_kernel(x_ref, nw_ref, sh_ref, sc_ref, wqh_ref, wql_ref, kh_ref, kl_ref,
                       h_ref, cnt_ref, r_ref, rank_ref, e2_ref, qt_ref, *, heads, topk):
    h = _norm_modulate(x_ref[...], nw_ref[...], sh_ref[0], sc_ref[0])
    hb, hl = _split_bf16(h)
    h_ref[...] = hb
    qt_ref[...] = _dot3(wqh_ref[0], wql_ref[0], hb, hl, NT_DIMS)
    nk = kh_ref.shape[2]
    tm = hb.shape[0]
    half = topk // 2
    row8 = lax.broadcasted_iota(jnp.int32, (half, tm), 0)

    def head_body(hd, carry):
        s = []
        for c in range(2):
            idx = hd * 2 + c
            qh, ql = _split_bf16(qt_ref[pl.ds(pl.multiple_of(idx * nk, nk), nk), :])
            s.append(_dot3(kh_ref[0, idx], kl_ref[0, idx], qh, ql, NN_DIMS))
        v1, _ = _top_values_ranked(s[0], topk, False)
        v2, rank2 = _top_values_ranked(s[1], topk, True)
        v1s = jnp.concatenate(v1, axis=0)
        v2s = jnp.concatenate(v2, axis=0)
        pieces = [v1[0] + v2s, v1[1] + v2s[0:half]]
        for a in range(2, half):
            pieces.append(jnp.where(row8 < topk // (a + 1), v1[a] + v2s[0:half], -jnp.inf))
        pieces.append(v1s[half:topk] + v2[0])
        cand = jnp.concatenate(pieces, axis=0)
        mx = v1[0] + v2[0]
        z = jnp.zeros_like(mx)
        tau = mx
        for it in range(topk):
            tau = jnp.max(cand, axis=0, keepdims=True)
            z = z + jnp.exp(tau - mx)
            if it + 1 < topk:
                cand = jnp.where(cand == tau, -jnp.inf, cand)
        cnt = jnp.zeros(s[0].shape, F32)
        for b in range(half):
            cnt = cnt + jnp.where(s[0] + v2[b] >= tau, 1.0, 0.0)
        extra = jnp.zeros_like(tau)
        for b in range(half, topk):
            extra = extra + jnp.where(v1[0] + v2[b] >= tau, 1.0, 0.0)
        cnt_ref[hd] = cnt + jnp.where(s[0] == v1[0], extra, 0.0)
        r_ref[hd] = jnp.exp(s[0] - v1[0]) / z
        rank_ref[hd] = rank2.astype(BF16)
        e2_ref[hd] = jnp.exp(s[1] - v2[0]).astype(BF16)
        return carry

    def pair_body(hp, carry):
        head_body(2 * hp, carry)
        return head_body(2 * hp + 1, carry)

    lax.fori_loop(0, heads // 2, pair_body, 0)


def peer_route(x, nw, sh, sc, wqt, keys2, layer, *, tm, rows_per_group):
    t, d = x.shape
    r = sh.shape[1]
    nq = wqt[0].shape[1]
    _, nhc, nk, dh = keys2[0].shape
    heads = nhc // 2
    tpg = rows_per_group // tm
    mod_spec = pl.BlockSpec((1, r, d), lambda i: (i // tpg, 0, 0))
    fac_spec = pl.BlockSpec((heads, nk, tm), lambda i: (0, 0, i))
    f32_shape = jax.ShapeDtypeStruct((heads, nk, t), F32)
    bf_shape = jax.ShapeDtypeStruct((heads, nk, t), BF16)
    wq_spec = pl.BlockSpec((1, nq, d), lambda i: (layer, 0, 0), pipeline_mode=pl.Buffered(1))
    key_spec = pl.BlockSpec((1, nhc, nk, dh), lambda i: (layer, 0, 0, 0), pipeline_mode=pl.Buffered(1))
    return pl.pallas_call(
        functools.partial(_peer_route_kernel, heads=heads, topk=PEER_TOPK),
        grid=(t // tm,),
        in_specs=[pl.BlockSpec((tm, d), lambda i: (i, 0)),
                  pl.BlockSpec((1, d), lambda i: (0, 0)),
                  mod_spec, mod_spec, wq_spec, wq_spec, key_spec, key_spec],
        out_specs=[pl.BlockSpec((tm, d), lambda i: (i, 0)), fac_spec, fac_spec, fac_spec, fac_spec],
        out_shape=[jax.ShapeDtypeStruct((t, d), BF16), f32_shape, f32_shape, bf_shape, bf_shape],
        scratch_shapes=[pltpu.VMEM((nq, tm), F32)],
        compiler_params=_cp(("parallel",)),
        name="peer_route",
    )(x, nw, sh, sc, wqt[0], wqt[1], keys2[0], keys2[1])


def _gelu_exact(a):
    return 0.5 * a * (1.0 + lax.erf(a * (1.0 / math.sqrt(2.0))))


def _peer_expert_kernel(h_ref, cnt_ref, r_ref, rank_ref, e2_ref, *rest, heads, ni, sub):
    ns = ni // sub
    u_refs, vt_refs = rest[:ns], rest[ns:2 * ns]
    x_ref, g_ref, o_ref, acc_ref, wa_ref = rest[2 * ns:]
    j = pl.program_id(1)
    nk = rank_ref.shape[1] * rank_ref.shape[2]

    @pl.when(j == 0)
    def _():
        acc_ref[...] = jnp.zeros_like(acc_ref)

    hb = h_ref[...]
    tm = hb.shape[0]
    pk = BF16_ROWS
    for c in range(ns):
        lo, hi = c * sub * nk, (c + 1) * sub * nk
        act = lax.dot_general(u_refs[c][0], hb, NT_DIMS, preferred_element_type=F32)
        act = _gelu_exact(act).astype(BF16)
        gs = []
        for ii in range(c * sub, (c + 1) * sub):
            g = None
            for hd in range(heads):
                cnt = jnp.broadcast_to(cnt_ref[hd, ii:ii + 1, :], (pk, tm)).astype(BF16)[None]
                rr = jnp.broadcast_to(r_ref[hd, ii:ii + 1, :], (pk, tm)).astype(BF16)[None]
                contrib = jnp.where(rank_ref[hd] < cnt, e2_ref[hd] * rr, jnp.zeros((), BF16))
                g = contrib if g is None else g + contrib
            gs.append(g.reshape(nk, tm))
        wa_ref[lo:hi, :] = jnp.concatenate(gs, axis=0) * act
    wa = wa_ref[...]
    dr = acc_ref.shape[0] // ns
    for c in range(ns):
        acc_ref[c * dr:(c + 1) * dr, :] += jnp.dot(vt_refs[c][0], wa, preferred_element_type=F32)

    @pl.when(j == pl.num_programs(1) - 1)
    def _():
        o_ref[...] = x_ref[...] + g_ref[0] * acc_ref[...].T


def peer_experts(h, cnt, rfac, rank, e2, u, vt, layer, x, gate, *, tm, ni, sub, rows_per_group):
    t, d = x.shape
    heads, nk, _ = cnt.shape
    r = gate.shape[1]
    tpg = rows_per_group // tm
    te = ni * nk
    row_spec = pl.BlockSpec((heads, ni, tm), lambda i, j: (0, j, i))
    full_spec = pl.BlockSpec((heads, nk // BF16_ROWS, BF16_ROWS, tm), lambda i, j: (0, 0, 0, i))
    rank = rank.reshape(heads, nk // BF16_ROWS, BF16_ROWS, t)
    e2 = e2.reshape(heads, nk // BF16_ROWS, BF16_ROWS, t)
    ns = ni // sub
    u_specs = [pl.BlockSpec((1, te // ns, d), lambda i, j, c=c: (layer, j * ns + c, 0)) for c in range(ns)]
    vt_specs = [pl.BlockSpec((1, d // ns, te), lambda i, j, c=c: (layer, c, j)) for c in range(ns)]
    return pl.pallas_call(
        functools.partial(_peer_expert_kernel, heads=heads, ni=ni, sub=sub),
        grid=(t // tm, nk // ni),
        in_specs=[pl.BlockSpec((tm, d), lambda i, j: (i, 0)),
                  row_spec, row_spec, full_spec, full_spec, *u_specs, *vt_specs,
                  pl.BlockSpec((tm, d), lambda i, j: (i, 0), pipeline_mode=pl.Buffered(1)),
                  pl.BlockSpec((1, r, d), lambda i, j: (i // tpg, 0, 0))],
        out_specs=pl.BlockSpec((tm, d), lambda i, j: (i, 0)),
        out_shape=jax.ShapeDtypeStruct((t, d), F32),
        scratch_shapes=[pltpu.VMEM((d, tm), F32), pltpu.VMEM((te, tm), BF16)],
        compiler_params=_cp(("parallel", "arbitrary")),
        name="peer_experts",
    )(h, cnt, rfac, rank, e2, *([u] * ns), *([vt] * ns), x, gate)


def _dilated_mult_table(tq, tk, n_off):
    o = jnp.arange(n_off, dtype=jnp.int32)[:, None, None]
    r = jnp.arange(tq, dtype=jnp.int32)[None, :, None]
    c = jnp.arange(tk, dtype=jnp.int32)[None, None, :]
    d = o * tk + r - c
    mult = jnp.zeros((n_off, tq, tk), F32)
    for win, dil in DILATED_PATTERNS:
        steps = win // dil
        ok = (d >= 0) & (d % dil == 0) & (d <= steps * dil)
        mult = mult + ok.astype(F32)
    return mult


def _dil_attn_kernel(q_ref, k_ref, v_ref, mult_ref, o_ref, m_ref, l_ref, acc_ref, *, t, n_off, scale):
    qi = pl.program_id(2)
    q = (q_ref[...].astype(F32) * scale).astype(BF16)
    m_ref[...] = jnp.full_like(m_ref, NEG_BIG)
    l_ref[...] = jnp.zeros_like(l_ref)
    acc_ref[...] = jnp.zeros_like(acc_ref)

    def body(kb, carry):
        off = qi - kb
        start = pl.multiple_of(kb * t, t)
        k = k_ref[pl.ds(start, t), :]
        v = v_ref[pl.ds(start, t), :]
        s = lax.dot_general(q, k, NT_DIMS, preferred_element_type=F32)
        mult = mult_ref[off]
        s = jnp.where(mult > 0.0, s, NEG_BIG)
        m_old = m_ref[...]
        m_new = jnp.maximum(m_old, jnp.max(s, axis=-1, keepdims=True))
        alpha = jnp.exp(m_old - m_new)
        p = jnp.exp(s - jnp.tile(m_new, (1, t // LANE))) * mult
        l_ref[...] = alpha * l_ref[...] + jnp.sum(p, axis=-1, keepdims=True)
        acc_ref[...] = alpha * acc_ref[...] + jnp.dot(p.astype(BF16), v, preferred_element_type=F32)
        m_ref[...] = m_new
        return carry

    lax.fori_loop(jnp.maximum(qi - (n_off - 1), 0), qi + 1, body, 0)
    o_ref[...] = (acc_ref[...] / l_ref[...]).astype(o_ref.dtype)


def dilated_attention_prompt(proj_bf, *, batch, seq, heads, t):
    win_max = min(max(w for w, _ in DILATED_PATTERNS), seq)
    n_off = min(-(-win_max // t) + 1, seq // t)
    mult = _dilated_mult_table(t, t, n_off)
    nq = seq // t
    kern = functools.partial(_dil_attn_kernel, t=t, n_off=n_off, scale=1.0 / math.sqrt(HD_A))
    return pl.pallas_call(
        kern,
        grid=(batch, heads, nq),
        in_specs=[pl.BlockSpec((t, HD_A), lambda b, h, i: (b * nq + i, h)),
                  pl.BlockSpec((seq, HD_A), lambda b, h, i: (b, heads + h)),
                  pl.BlockSpec((seq, HD_A), lambda b, h, i: (b, 2 * heads + h)),
                  pl.BlockSpec((n_off, t, t), lambda b, h, i: (0, 0, 0))],
        out_specs=pl.BlockSpec((t, HD_A), lambda b, h, i: (b * nq + i, h)),
        out_shape=jax.ShapeDtypeStruct((batch * seq, heads * HD_A), BF16),
        scratch_shapes=[pltpu.VMEM((t, LANE), F32), pltpu.VMEM((t, LANE), F32), pltpu.VMEM((t, HD_A), F32)],
        compiler_params=_cp(("parallel", "parallel", "arbitrary")),
        name="dilated_attention",
    )(proj_bf, proj_bf, proj_bf, mult)


def _pair_cols(q, p, lane_lo):
    a = q[:, 2 * p:2 * p + 1]
    b = q[:, 2 * p + 1:2 * p + 2]
    return jnp.where(lane_lo, a, b)


def _ssd_kernel(z_ref, xbc_ref, dt_ref, cw_ref, cb_ref, dtb_ref, alog_ref, dsk_ref, nw_ref,
                y_ref, st_ref, xprev_ref, s_ref, *, d_inner, n_pairs, conv_w):
    c = pl.program_id(1)
    q = xbc_ref.shape[0]
    pre = SUBLANE

    @pl.when(c == 0)
    def _():
        xprev_ref[0:pre, :] = jnp.zeros((pre, xprev_ref.shape[1]), F32)
        s_ref[...] = jnp.zeros_like(s_ref)

    x_in = xbc_ref[...]
    xprev_ref[pre:pre + q, :] = x_in
    conv = cb_ref[...]
    for kk in range(conv_w):
        conv = conv + xprev_ref[pl.ds(pre - (conv_w - 1) + kk, q), :] * cw_ref[kk:kk + 1, :]
    xprev_ref[0:pre, :] = x_in[q - pre:q, :]
    act = _silu(conv)
    gn = SSM_GROUPS * D_STATE
    xs = act[:, :d_inner]
    bm = act[:, d_inner:d_inner + gn]
    cm = act[:, d_inner + gn:d_inner + 2 * gn]

    dt = _softplus(dt_ref[...] + dtb_ref[...])
    a = -jnp.exp(alog_ref[...])
    dta = dt * a
    row = lax.broadcasted_iota(jnp.int32, (q, q), 0)
    col = lax.broadcasted_iota(jnp.int32, (q, q), 1)
    causal = row >= col
    cum = jnp.dot(causal.astype(F32), dta, preferred_element_type=F32, precision=lax.Precision.HIGHEST)
    cum_t = cum.T
    dt_t = dt.T
    cum_end = cum[q - 1:q, :]
    wdec = jnp.exp(cum_end - cum) * dt
    ecum = jnp.exp(cum)
    cdec = jnp.exp(cum_end)
    lane_lo = lax.broadcasted_iota(jnp.int32, (q, LANE), 1) < SSM_HEADDIM
    lane_lo1 = lane_lo[0:1, :]
    pairs_per_group = (n_pairs * 2 // SSM_GROUPS) // 2

    ys = []
    for g in range(SSM_GROUPS):
        bg = bm[:, g * D_STATE:(g + 1) * D_STATE]
        cg = cm[:, g * D_STATE:(g + 1) * D_STATE].astype(BF16)
        cb = lax.dot_general(cg, bg.astype(BF16), NT_DIMS, preferred_element_type=F32)
        bg_t = bg.T.astype(BF16)
        for pp in range(pairs_per_group):
            p = g * pairs_per_group + pp
            xp = xs[:, p * LANE:(p + 1) * LANE]
            xpb = xp.astype(BF16)
            y_intra = None
            for hh in range(2):
                h = 2 * p + hh
                seg = cum[:, h:h + 1] - cum_t[h:h + 1, :]
                dec = jnp.exp(jnp.where(causal, seg, -jnp.inf))
                mm = (cb * dec * dt_t[h:h + 1, :]).astype(BF16)
                yh = jnp.dot(mm, xpb, preferred_element_type=F32)
                y_intra = yh if y_intra is None else jnp.where(lane_lo, y_intra, yh)
            sp = s_ref[p]
            y_inter = jnp.dot(cg, sp.astype(BF16), preferred_element_type=F32) * _pair_cols(ecum, p, lane_lo)
            ys.append(y_intra + y_inter + dsk_ref[:, p * LANE:(p + 1) * LANE] * xp)
            xw = (xp * _pair_cols(wdec, p, lane_lo)).astype(BF16)
            s_ref[p] = (_pair_cols(cdec, p, lane_lo1) * sp
                        + jnp.dot(bg_t, xw, preferred_element_type=F32))
    y = jnp.concatenate(ys, axis=1)
    y = y * _silu(z_ref[...])
    gw = d_inner // SSM_GROUPS
    outs = []
    for g in range(SSM_GROUPS):
        yg = y[:, g * gw:(g + 1) * gw]
        outs.append(yg * lax.rsqrt(jnp.mean(yg * yg, axis=-1, keepdims=True) + EPS))
    y_ref[...] = (jnp.concatenate(outs, axis=1) * nw_ref[...]).astype(y_ref.dtype)

    @pl.when(c == pl.num_programs(1) - 1)
    def _():
        st_ref[0] = s_ref[...]


def ssd_prompt(proj, conv_w, conv_b, dtb_pad, alog_pad, dskip_lane, norm_w, *, batch, seq, d_inner, col0):
    q = SSD_CHUNK
    nc = seq // q
    conv_dim = conv_w.shape[1]
    cw = conv_w.shape[0]
    n_pairs = d_inner // (2 * SSM_HEADDIM)
    zb = col0 // d_inner
    xb = (col0 + d_inner) // conv_dim
    db = (col0 + d_inner + conv_dim) // LANE
    kern = functools.partial(_ssd_kernel, d_inner=d_inner, n_pairs=n_pairs, conv_w=cw)
    const = lambda b, c: (0, 0)
    return pl.pallas_call(
        kern,
        grid=(batch, nc),
        in_specs=[pl.BlockSpec((q, d_inner), lambda b, c: (b * nc + c, zb)),
                  pl.BlockSpec((q, conv_dim), lambda b, c: (b * nc + c, xb)),
                  pl.BlockSpec((q, LANE), lambda b, c: (b * nc + c, db)),
                  pl.BlockSpec((cw, conv_dim), const),
                  pl.BlockSpec((1, conv_dim), const),
                  pl.BlockSpec((1, LANE), const),
                  pl.BlockSpec((1, LANE), const),
                  pl.BlockSpec((1, d_inner), const),
                  pl.BlockSpec((1, d_inner), const)],
        out_specs=[pl.BlockSpec((q, d_inner), lambda b, c: (b * nc + c, 0)),
                   pl.BlockSpec((1, n_pairs, D_STATE, LANE), lambda b, c: (b, 0, 0, 0))],
        out_shape=[jax.ShapeDtypeStruct((batch * seq, d_inner), BF16),
                   jax.ShapeDtypeStruct((batch, n_pairs, D_STATE, LANE), F32)],
        scratch_shapes=[pltpu.VMEM((SUBLANE + q, conv_dim), F32),
                        pltpu.VMEM((n_pairs, D_STATE, LANE), F32)],
        compiler_params=_cp(("parallel", "arbitrary")),
        name="ssd_prompt",
    )(proj, proj, proj, conv_w, conv_b, dtb_pad, alog_pad, dskip_lane, norm_w)


def _state_from_pairs(st):
    b, npair, n, _ = st.shape
    return st.reshape(b, npair, n, 2, SSM_HEADDIM).transpose(0, 1, 3, 4, 2).reshape(b, 2 * npair, SSM_HEADDIM, n)


def _diff_lambda(lamp, lam_init):
    l1 = jnp.sum(lamp[0:1] * lamp[1:2], axis=-1, keepdims=True)
    l2 = jnp.sum(lamp[2:3] * lamp[3:4], axis=-1, keepdims=True)
    return jnp.exp(l1) - jnp.exp(l2) + lam_init


def _diff_finish(acc0, l0, acc1, l1, lam, sw, lam_init):
    o = acc0 / l0 - lam * (acc1 / l1)
    o = o * lax.rsqrt(jnp.mean(o * o, axis=-1, keepdims=True) + EPS) * sw
    return o * (1.0 - lam_init)


def _diff_attn_kernel(lamp_ref, sw_ref, q_ref, k_ref, v_ref, o_ref, m_ref, l_ref, acc_ref, *, t, scale, lam_init):
    qi = pl.program_id(2)
    q = (q_ref[...].astype(F32) * scale).astype(BF16)
    m_ref[...] = jnp.full_like(m_ref, NEG_BIG)
    l_ref[...] = jnp.zeros_like(l_ref)
    acc_ref[...] = jnp.zeros_like(acc_ref)
    row = lax.broadcasted_iota(jnp.int32, (t, t), 0)
    col = lax.broadcasted_iota(jnp.int32, (t, t), 1)

    def step(kb, masked):
        start = pl.multiple_of(kb * t, t)
        k = k_ref[pl.ds(start, t), :]
        v = v_ref[pl.ds(start, t), :]
        for m in range(2):
            s = lax.dot_general(q[:, m * HD_C:(m + 1) * HD_C], k[:, m * HD_C:(m + 1) * HD_C], NT_DIMS,
                                preferred_element_type=F32)
            if masked:
                s = jnp.where(row >= col, s, NEG_BIG)
            m_old = m_ref[m]
            m_new = jnp.maximum(m_old, jnp.max(s, axis=-1, keepdims=True))
            alpha = jnp.exp(m_old - m_new)
            p = jnp.exp(s - jnp.tile(m_new, (1, t // LANE)))
            l_ref[m] = alpha * l_ref[m] + jnp.sum(p, axis=-1, keepdims=True)
            acc_ref[m] = (jnp.tile(alpha, (1, v.shape[1] // LANE)) * acc_ref[m]
                          + jnp.dot(p.astype(BF16), v, preferred_element_type=F32))
            m_ref[m] = m_new

    def body(kb, carry):
        step(kb, False)
        return carry

    lax.fori_loop(0, qi, body, 0)
    step(qi, True)
    lam = _diff_lambda(lamp_ref[...], lam_init)
    o_ref[...] = _diff_finish(acc_ref[0], l_ref[0][:, 0:1], acc_ref[1], l_ref[1][:, 0:1], lam, sw_ref[...],
                              lam_init).astype(o_ref.dtype)


def diff_attention_prompt(proj_bf, lamp, subln_w, *, batch, seq, heads, t, lam_init):
    hw = 2 * HD_C
    nq = seq // t
    kern = functools.partial(_diff_attn_kernel, t=t, scale=1.0 / math.sqrt(HD_C), lam_init=lam_init)
    return pl.pallas_call(
        kern,
        grid=(batch, heads, nq),
        in_specs=[pl.BlockSpec((4, HD_C), lambda b, h, i: (0, 0)),
                  pl.BlockSpec((1, hw), lambda b, h, i: (0, 0)),
                  pl.BlockSpec((t, hw), lambda b, h, i: (b * nq + i, h)),
                  pl.BlockSpec((seq, hw), lambda b, h, i: (b, heads + h)),
                  pl.BlockSpec((seq, hw), lambda b, h, i: (b, 2 * heads + h))],
        out_specs=pl.BlockSpec((t, hw), lambda b, h, i: (b * nq + i, h)),
        out_shape=jax.ShapeDtypeStruct((batch * seq, heads * hw), BF16),
        scratch_shapes=[pltpu.VMEM((2, t, LANE), F32), pltpu.VMEM((2, t, LANE), F32), pltpu.VMEM((2, t, hw), F32)],
        compiler_params=_cp(("parallel", "parallel", "arbitrary")),
        name="diff_attention",
    )(lamp, subln_w, proj_bf, proj_bf, proj_bf)


def _diff_sample_kernel(pt_ref, q_ref, kn_ref, vn_ref, lamp_ref, sw_ref, *rest, pp, scale, lam_init):
    del pt_ref
    k_refs = rest[:pp]
    v_refs = rest[pp:2 * pp]
    o_ref, m_ref, l_ref, acc_ref = rest[2 * pp:]
    pg = pl.program_id(1)
    q = q_ref[0] * scale

    @pl.when(pg == 0)
    def _():
        prod = q * kn_ref[0]
        for m in range(2):
            m_ref[m] = jnp.sum(prod[:, m * HD_C:(m + 1) * HD_C], axis=-1, keepdims=True)
            l_ref[m] = jnp.ones_like(l_ref[m])
            acc_ref[m] = vn_ref[0]

    for i in range(pp):
        kk = k_refs[i][0, 0]
        vv = v_refs[i][0, 0]
        prod = kk * q[None]
        for m in range(2):
            s = jnp.sum(prod[:, :, m * HD_C:(m + 1) * HD_C], axis=-1, keepdims=True)
            m_old = m_ref[m]
            m_new = jnp.maximum(m_old, jnp.max(s, axis=0))
            alpha = jnp.exp(m_old - m_new)
            p = jnp.exp(s - m_new[None])
            l_ref[m] = alpha * l_ref[m] + jnp.sum(p, axis=0)
            acc_ref[m] = alpha * acc_ref[m] + jnp.sum(p * vv, axis=0)
            m_ref[m] = m_new

    @pl.when(pg == pl.num_programs(1) - 1)
    def _():
        lam = _diff_lambda(lamp_ref[...], lam_init)
        o_ref[0] = _diff_finish(acc_ref[0], l_ref[0], acc_ref[1], l_ref[1], lam, sw_ref[...], lam_init)


def diff_attention_sample(page_table, q3, kn3, vn3, lamp, subln_w, cache_k, cache_v, *, layer, pp, lam_init):
    bsz, heads, hw = q3.shape
    n_pages = page_table.shape[1]
    page = cache_k.shape[2]
    row_spec = pl.BlockSpec((1, heads, hw), lambda b, p, pt: (b, 0, 0))

    def page_spec(i):
        return pl.BlockSpec((1, 1, page, heads, hw), lambda b, p, pt: (layer, pt[b, p * pp + i], 0, 0, 0))

    kern = functools.partial(_diff_sample_kernel, pp=pp, scale=1.0 / math.sqrt(HD_C), lam_init=lam_init)
    grid_spec = pltpu.PrefetchScalarGridSpec(
        num_scalar_prefetch=1,
        grid=(bsz, n_pages // pp),
        in_specs=[row_spec, row_spec, row_spec,
                  pl.BlockSpec((4, HD_C), lambda b, p, pt: (0, 0)),
                  pl.BlockSpec((1, hw), lambda b, p, pt: (0, 0))]
                 + [page_spec(i) for i in range(pp)] + [page_spec(i) for i in range(pp)],
        out_specs=row_spec,
        scratch_shapes=[pltpu.VMEM((2, heads, 1), F32), pltpu.VMEM((2, heads, 1), F32),
                        pltpu.VMEM((2, heads, hw), F32)],
    )
    return pl.pallas_call(
        kern,
        grid_spec=grid_spec,
        out_shape=jax.ShapeDtypeStruct((bsz, heads, hw), F32),
        compiler_params=_cp(("parallel", "arbitrary")),
        name="diff_attention_sample",
    )(page_table, q3, kn3, vn3, lamp, subln_w, *([cache_k] * pp), *([cache_v] * pp))


def _win_sample_kernel(q_ref, kn_ref, vn_ref, ck_ref, cv_ref, o_ref, m_ref, l_ref, acc_ref, *, tr, wb, scale):
    r = pl.program_id(1)
    q = q_ref[0] * scale

    @pl.when(r == 0)
    def _():
        m_ref[...] = jnp.sum(q * kn_ref[0], axis=-1, keepdims=True)
        l_ref[...] = jnp.full_like(l_ref, float(len(DILATED_PATTERNS)))
        acc_ref[...] = float(len(DILATED_PATTERNS)) * vn_ref[0]

    kk = ck_ref[0, 0]
    s = jnp.sum(kk * q[None], axis=-1, keepdims=True)
    j = r * tr + lax.broadcasted_iota(jnp.int32, s.shape, 0)
    d = wb - j
    mult = jnp.zeros(s.shape, F32)
    for win, dil in DILATED_PATTERNS:
        steps = win // dil
        mult = mult + ((d % dil == 0) & (d <= steps * dil)).astype(F32)
    s = jnp.where(mult > 0.0, s, NEG_BIG)
    m_old = m_ref[...]
    m_new = jnp.maximum(m_old, jnp.max(s, axis=0))
    alpha = jnp.exp(m_old - m_new)
    p = jnp.exp(s - m_new[None]) * mult
    l_ref[...] = alpha * l_ref[...] + jnp.sum(p, axis=0)
    acc_ref[...] = alpha * acc_ref[...] + jnp.sum(p * cv_ref[0, 0], axis=0)
    m_ref[...] = m_new

    @pl.when(r == pl.num_programs(1) - 1)
    def _():
        o_ref[0] = acc_ref[...] / l_ref[...]


def dilated_attention_sample(q3, kn3, vn3, cache_k, cache_v, layer, *, tr):
    bsz, heads, hd = q3.shape
    wb = cache_k.shape[2]
    row_spec = pl.BlockSpec((1, heads, hd), lambda b, r: (b, 0, 0))
    c_spec = pl.BlockSpec((1, 1, tr, heads, hd), lambda b, r: (layer, b, r, 0, 0))
    kern = functools.partial(_win_sample_kernel, tr=tr, wb=wb, scale=1.0 / math.sqrt(hd))
    return pl.pallas_call(
        kern,
        grid=(bsz, wb // tr),
        in_specs=[row_spec, row_spec, row_spec, c_spec, c_spec],
        out_specs=row_spec,
        out_shape=jax.ShapeDtypeStruct((bsz, heads, hd), F32),
        scratch_shapes=[pltpu.VMEM((heads, 1), F32), pltpu.VMEM((heads, 1), F32), pltpu.VMEM((heads, hd), F32)],
        compiler_params=_cp(("parallel", "arbitrary")),
        name="dilated_attention_sample",
    )(q3, kn3, vn3, cache_k, cache_v)


def _lane_to_col(v):
    n = v.shape[1] // LANE
    return jnp.concatenate([jnp.broadcast_to(v[:, k * LANE:(k + 1) * LANE], (LANE, LANE)).T for k in range(n)], axis=0)


def _col_to_lane(c):
    n = c.shape[0] // LANE
    return jnp.concatenate([jnp.broadcast_to(c[k * LANE:(k + 1) * LANE], (LANE, LANE)).T[0:1] for k in range(n)], axis=1)


def _ssd_step_kernel(z_ref, xbc_ref, dt_ref, cs_ref, st_ref, cw_ref, cb_ref, dtb_ref, alog_ref, dsk_ref, nw_ref,
                     e_ref, y_ref, cso_ref, sto_ref, *, d_inner):
    x = xbc_ref[0]
    cs = cs_ref[0]
    ncs = cs.shape[0]
    conv = cb_ref[...] + x * cw_ref[ncs:ncs + 1, :]
    for kk in range(ncs):
        conv = conv + cs[kk:kk + 1, :] * cw_ref[kk:kk + 1, :]
    cso_ref[0, 0:ncs - 1, :] = cs[1:ncs, :]
    cso_ref[0, ncs - 1:ncs, :] = x
    act = _silu(conv)
    gn = SSM_GROUPS * D_STATE
    xs = act[:, :d_inner]
    bm = act[:, d_inner:d_inner + gn]
    cm = act[:, d_inner + gn:d_inner + 2 * gn]
    dt = _softplus(dt_ref[0] + dtb_ref[...])
    dt_lane = jnp.dot(jnp.broadcast_to(dt, (SUBLANE, LANE)), e_ref[...], preferred_element_type=F32,
                      precision=lax.Precision.HIGHEST)[0:1]
    dec_lane = jnp.exp(dt_lane * (-jnp.exp(alog_ref[...])))
    dec_col = _lane_to_col(dec_lane)
    dtx_col = _lane_to_col(dt_lane * xs)
    rows_g = d_inner // SSM_GROUPS
    b_rows = jnp.concatenate([jnp.broadcast_to(bm[:, g * D_STATE:(g + 1) * D_STATE], (rows_g, D_STATE))
                              for g in range(SSM_GROUPS)], axis=0)
    c_rows = jnp.concatenate([jnp.broadcast_to(cm[:, g * D_STATE:(g + 1) * D_STATE], (rows_g, D_STATE))
                              for g in range(SSM_GROUPS)], axis=0)
    st = st_ref[0].reshape(d_inner, D_STATE)
    st_new = dec_col * st + dtx_col * b_rows
    sto_ref[0] = st_new.reshape(sto_ref.shape[1:])
    y = _col_to_lane(jnp.sum(st_new * c_rows, axis=-1, keepdims=True))
    y = y + dsk_ref[...] * xs
    y = y * _silu(z_ref[0])
    gw = d_inner // SSM_GROUPS
    outs = []
    for g in range(SSM_GROUPS):
        yg = y[:, g * gw:(g + 1) * gw]
        outs.append(yg * lax.rsqrt(jnp.mean(yg * yg, axis=-1, keepdims=True) + EPS))
    y_ref[0] = jnp.concatenate(outs, axis=1) * nw_ref[...]


def ssd_step(z3, xbc3, dt3, conv_state, ssm_state, conv_w, conv_b, dtb_pad, alog_lane, dskip_lane, norm_w, expand):
    bsz, _, d_inner = z3.shape
    conv_dim = xbc3.shape[2]
    ncs = conv_state.shape[1]
    hb, hp, n = ssm_state.shape[1:]
    const = lambda b: (0, 0)
    kern = functools.partial(_ssd_step_kernel, d_inner=d_inner)
    return pl.pallas_call(
        kern,
        grid=(bsz,),
        in_specs=[pl.BlockSpec((1, 1, d_inner), lambda b: (b, 0, 0)),
                  pl.BlockSpec((1, 1, conv_dim), lambda b: (b, 0, 0)),
                  pl.BlockSpec((1, 1, LANE), lambda b: (b, 0, 0)),
                  pl.BlockSpec((1, ncs, conv_dim), lambda b: (b, 0, 0)),
                  pl.BlockSpec((1, hb, hp, n), lambda b: (b, 0, 0, 0)),
                  pl.BlockSpec((ncs + 1, conv_dim), const),
                  pl.BlockSpec((1, conv_dim), const),
                  pl.BlockSpec((1, LANE), const),
                  pl.BlockSpec((1, d_inner), const),
                  pl.BlockSpec((1, d_inner), const),
                  pl.BlockSpec((1, d_inner), const),
                  pl.BlockSpec((LANE, d_inner), const)],
        out_specs=[pl.BlockSpec((1, 1, d_inner), lambda b: (b, 0, 0)),
                   pl.BlockSpec((1, ncs, conv_dim), lambda b: (b, 0, 0)),
                   pl.BlockSpec((1, hb, hp, n), lambda b: (b, 0, 0, 0))],
        out_shape=[jax.ShapeDtypeStruct((bsz, 1, d_inner), F32),
                   jax.ShapeDtypeStruct(conv_state.shape, F32),
                   jax.ShapeDtypeStruct(ssm_state.shape, F32)],
        compiler_params=_cp(("parallel",)),
        name="ssd_step",
    )(z3, xbc3, dt3, conv_state, ssm_state, conv_w, conv_b, dtb_pad, alog_lane, dskip_lane, norm_w, expand)


PROMPT_TM = 512
NORMMOD_TM = 1024
ROUTE_TM = 256
EXPERT_NI = 8
EXPERT_SUB = 2
ATTN_T = 512
SAMPLE_PAD = LANE
WIN_SAMPLE_ROWS = 512
PAGES_PER_STEP = 8


def _pad_cols(w, n):
    return jnp.pad(w, ((0, 0), (0, n - w.shape[1])))


def _pad_rows(a, n):
    return jnp.pad(a, ((0, n - a.shape[0]),) + ((0, 0),) * (a.ndim - 1))


def _round_up(n, m):
    return -(-n // m) * m


def kernel(x_prompt, x_sample, c_prompt, c_sample, cache_win_k, cache_win_v, state_conv, state_ssm,
           cache_diff_k, cache_diff_v, page_table, norm1_w, norm2_w, w_ada, b_ada, w_in_even, w_out_even,
           conv_w, conv_b, dt_bias, a_log, d_skip, ssm_norm_w, w_in_odd, w_out_odd, lambda_q1, lambda_k1,
           lambda_q2, lambda_k2, subln_w, peer_wq, peer_keys, peer_u, peer_v, final_norm_w):
    bsz, seq, d = x_prompt.shape
    sb, st, _ = x_sample.shape
    assert st == 1, "one new token per sample sequence"
    depth = w_ada.shape[0]
    h_a = cache_win_k.shape[3]
    a_width = h_a * HD_A
    d_inner = ssm_norm_w.shape[1]
    conv_dim = conv_w.shape[2]
    h_b = dt_bias.shape[1]
    h_c = cache_diff_k.shape[3]
    c_width = h_c * 2 * HD_C
    wb_p = min(max(w for w, _ in DILATED_PATTERNS), seq)
    ncs = conv_w.shape[1] - 1

    xp = x_prompt.reshape(bsz * seq, d)
    xs = x_sample.reshape(sb, d)

    n_c = _round_up(bsz + sb, SUBLANE)
    c_all = _pad_rows(jnp.concatenate([c_prompt, c_sample], axis=0), n_c)
    mod = adaln_all(c_all, w_ada, b_ada)

    expand = (jnp.arange(LANE)[:, None] == (jnp.arange(d_inner) // SSM_HEADDIM)[None, :]).astype(F32)

    wq_all = _split_bf16(jnp.swapaxes(peer_wq, 1, 2))
    nhc = peer_keys.shape[1] * peer_keys.shape[2]
    keys_all = _split_bf16(peer_keys.reshape(depth, nhc, peer_keys.shape[3], peer_keys.shape[4]))
    u_all = peer_u.astype(BF16)
    vt_all = jnp.swapaxes(peer_v, 1, 2).astype(BF16)

    outs = {k: [] for k in ("pwk", "pwv", "pcv", "pss", "swk", "swv", "scv", "sss", "sdk", "sdv")}
    pdk_buf = jnp.zeros((depth // 2, bsz * seq, c_width), F32)
    pdv_buf = jnp.zeros((depth // 2, bsz * seq, c_width), F32)
    for l in range(depth):
        parts = [mod[l, :, i * d:(i + 1) * d] for i in range(6)]
        pm = [p[:bsz][:, None, :] for p in parts]
        sm = [p[bsz:bsz + sb][None] for p in parts]
        n1 = norm1_w[l][None]
        n2 = norm2_w[l][None]
        if l % 2 == 0:
            e = l // 2
            even_in = w_in_even.shape[2]
            n_pad = _round_up(even_in, 7 * LANE)
            w_in = _pad_cols(w_in_even[e], n_pad).astype(BF16)
            tn = n_pad // 7
            wo = w_out_even[e].astype(BF16)
            dtb_pad = _pad_cols(dt_bias[e][None], LANE)
            alog_pad = _pad_cols(a_log[e][None], LANE)
            alog_lane = jnp.repeat(a_log[e], SSM_HEADDIM)[None]
            dsk_lane = jnp.repeat(d_skip[e], SSM_HEADDIM)[None]
            cw, cb, nw = conv_w[e], conv_b[e][None], ssm_norm_w[e][None]
            col_z = 3 * a_width

            proj, proj_bf = normmod_matmul(xp, n1, pm[0], pm[1], w_in, tm=min(NORMMOD_TM, seq), tn=tn,
                                           rows_per_group=seq)
            ya = dilated_attention_prompt(proj_bf, batch=bsz, seq=seq, heads=h_a, t=ATTN_T)
            yb, st_pairs = ssd_prompt(proj, cw, cb, dtb_pad, alog_pad, dsk_lane, nw,
                                      batch=bsz, seq=seq, d_inner=d_inner, col0=col_z)
            xp = matmul_residual([ya, yb], [wo[:a_width], wo[a_width:]], xp, pm[2],
                                 tm=PROMPT_TM, tn=d, rows_per_group=seq)
            proj3 = proj.reshape(bsz, seq, n_pad)
            outs["pwk"].append(proj3[:, seq - wb_p:, a_width:2 * a_width].reshape(bsz, wb_p, h_a, HD_A))
            outs["pwv"].append(proj3[:, seq - wb_p:, 2 * a_width:3 * a_width].reshape(bsz, wb_p, h_a, HD_A))
            outs["pcv"].append(proj3[:, seq - ncs:, col_z + d_inner:col_z + d_inner + conv_dim])
            outs["pss"].append(_state_from_pairs(st_pairs))

            sproj, _ = normmod_matmul(xs, n1, sm[0], sm[1], w_in, tm=sb, tn=tn, rows_per_group=sb)
            q3 = sproj[:, :a_width].reshape(sb, h_a, HD_A)
            k3 = sproj[:, a_width:2 * a_width].reshape(sb, h_a, HD_A)
            v3 = sproj[:, 2 * a_width:3 * a_width].reshape(sb, h_a, HD_A)
            ya_s = dilated_attention_sample(q3, k3, v3, cache_win_k, cache_win_v, e,
                                            tr=min(WIN_SAMPLE_ROWS, cache_win_k.shape[2]))
            z3 = sproj[:, col_z:col_z + d_inner][:, None, :]
            xbc3 = sproj[:, col_z + d_inner:col_z + d_inner + conv_dim][:, None, :]
            dt3 = sproj[:, col_z + d_inner + conv_dim:col_z + d_inner + conv_dim + LANE][:, None, :]
            yb_s, cs_new, ss_new = ssd_step(z3, xbc3, dt3, state_conv[e], state_ssm[e], cw, cb, dtb_pad,
                                            alog_lane, dsk_lane, nw, expand)
            xs = matmul_residual([ya_s.reshape(sb, a_width).astype(BF16), yb_s.reshape(sb, d_inner).astype(BF16)],
                                 [wo[:a_width], wo[a_width:]], xs, sm[2], tm=sb, tn=1024, rows_per_group=sb)
            outs["swk"].append(k3[:, None])
            outs["swv"].append(v3[:, None])
            outs["scv"].append(cs_new)
            outs["sss"].append(ss_new)
        else:
            o_ = l // 2
            lam_init = 0.8 - 0.6 * math.exp(-0.3 * l)
            w_in = w_in_odd[o_].astype(BF16)
            wo = w_out_odd[o_].astype(BF16)
            lamp = jnp.stack([lambda_q1[o_], lambda_k1[o_], lambda_q2[o_], lambda_k2[o_]])
            sw = subln_w[o_][None]

            proj_bf, pdk_buf, pdv_buf = normmod_matmul_kv(xp, n1, pm[0], pm[1], w_in, pdk_buf, pdv_buf, o_,
                                                          tm=PROMPT_TM, tn=1024, rows_per_group=seq)
            o = diff_attention_prompt(proj_bf, lamp, sw, batch=bsz, seq=seq, heads=h_c, t=ATTN_T, lam_init=lam_init)
            xp = matmul_residual([o], [wo], xp, pm[2], tm=PROMPT_TM, tn=d, rows_per_group=seq)

            sproj, _ = normmod_matmul(xs, n1, sm[0], sm[1], w_in, tm=sb, tn=1024, rows_per_group=sb)
            q3 = sproj[:, :c_width].reshape(sb, h_c, 2 * HD_C)
            k3 = sproj[:, c_width:2 * c_width].reshape(sb, h_c, 2 * HD_C)
            v3 = sproj[:, 2 * c_width:3 * c_width].reshape(sb, h_c, 2 * HD_C)
            o_s = diff_attention_sample(page_table, q3, k3, v3, lamp, sw, cache_diff_k, cache_diff_v,
                                        layer=o_, pp=PAGES_PER_STEP, lam_init=lam_init)
            xs = matmul_residual([o_s.reshape(sb, c_width).astype(BF16)], [wo], xs, sm[2],
                                 tm=sb, tn=1024, rows_per_group=sb)
            outs["sdk"].append(k3[:, None])
            outs["sdv"].append(v3[:, None])

        routed = peer_route(xp, n2, pm[3], pm[4], wq_all, keys_all, l, tm=ROUTE_TM, rows_per_group=seq)
        xp = peer_experts(*routed, u_all, vt_all, l, xp, pm[5], tm=PROMPT_TM, ni=EXPERT_NI, sub=EXPERT_SUB,
                          rows_per_group=seq)

        xs_pad = _pad_rows(xs, SAMPLE_PAD)
        smp = [jnp.pad(m_, ((0, 0), (0, SAMPLE_PAD - sb), (0, 0))) for m_ in sm[3:6]]
        routed = peer_route(xs_pad, n2, smp[0], smp[1], wq_all, keys_all, l, tm=SAMPLE_PAD, rows_per_group=SAMPLE_PAD)
        xs = peer_experts(*routed, u_all, vt_all, l, xs_pad, smp[2], tm=SAMPLE_PAD, ni=EXPERT_NI, sub=EXPERT_SUB,
                          rows_per_group=SAMPLE_PAD)[:sb]

    y_prompt = rmsnorm_rows(xp, final_norm_w[None], tm=PROMPT_TM).reshape(bsz, seq, d)
    y_sample = rmsnorm_rows(xs, final_norm_w[None], tm=sb).reshape(sb, st, d)
    stk = lambda k: jnp.stack(outs[k])
    kv_shape = (depth // 2, bsz, seq, h_c, 2 * HD_C)
    return (y_prompt, y_sample, stk("pwk"), stk("pwv"), stk("pcv"), stk("pss"),
            pdk_buf.reshape(kv_shape), pdv_buf.reshape(kv_shape),
            stk("swk"), stk("swv"), stk("scv"), stk("sss"), stk("sdk"), stk("sdv"))
```

```python
import functools
import math

import jax
import jax.numpy as jnp
from jax import lax
from jax.experimental import pallas as pl
from jax.experimental.pallas import tpu as pltpu

F32 = jnp.float32
BF16 = jnp.bfloat16
EPS = 1e-6
NEG_BIG = -1e30

HD_A = 128
HD_C = 128
DILATED_PATTERNS = ((128, 1), (512, 4), (2048, 16))
SSM_HEADDIM = 64
SSM_GROUPS = 4
D_STATE = 128
SSD_CHUNK = 128
PEER_HEADS = 8
PEER_TOPK = 16
ROUTE_HEADS_PER_TRIP = 4

V7X_VMEM_LIMIT = 58 * 1024 * 1024
LANE = 128
SUBLANE = 8
BF16_ROWS = 16

NT_DIMS = (((1,), (1,)), ((), ()))


def _cp(sem, vmem=V7X_VMEM_LIMIT):
    return pltpu.CompilerParams(dimension_semantics=sem, vmem_limit_bytes=vmem)


def _sigmoid(x):
    return 1.0 / (1.0 + jnp.exp(-x))


def _silu(x):
    return x * _sigmoid(x)


def _softplus(x):
    return jnp.maximum(x, 0.0) + jnp.log1p(jnp.exp(-jnp.abs(x)))


def _norm_modulate(x, nw, sh, sc):
    ms = jnp.mean(x * x, axis=-1, keepdims=True)
    y = x * lax.rsqrt(ms + EPS) * nw
    return y * (1.0 + sc) + sh


def _adaln_kernel(c_ref, w_ref, b_ref, o_ref):
    a_hi, a_lo = _split_bf16(_silu(c_ref[...]))
    w_hi, w_lo = _split_bf16(w_ref[0])
    o_ref[0] = _dot3(a_hi, a_lo, w_hi, w_lo, NN_DIMS) + b_ref[0]


def adaln_all(c_all, w_ada, b_ada):
    nl, d, n = w_ada.shape
    r = c_all.shape[0]
    tn = 1024
    return pl.pallas_call(
        _adaln_kernel,
        grid=(nl, n // tn),
        in_specs=[pl.BlockSpec((r, d), lambda l, j: (0, 0)),
                  pl.BlockSpec((1, d, tn), lambda l, j: (l, 0, j)),
                  pl.BlockSpec((1, 1, tn), lambda l, j: (l, 0, j))],
        out_specs=pl.BlockSpec((1, r, tn), lambda l, j: (l, 0, j)),
        out_shape=jax.ShapeDtypeStruct((nl, r, n), F32),
        compiler_params=_cp(("parallel", "parallel")),
        name="adaln",
    )(c_all, w_ada, b_ada.reshape(nl, 1, n))


def _normmod_mm_kernel(x_ref, nw_ref, sh_ref, sc_ref, w_ref, o_ref, ob_ref, h_ref):
    @pl.when(pl.program_id(1) == 0)
    def _():
        h_ref[...] = _norm_modulate(x_ref[...], nw_ref[...], sh_ref[0], sc_ref[0]).astype(BF16)

    o = jnp.dot(h_ref[...], w_ref[...], preferred_element_type=F32)
    o_ref[...] = o
    ob_ref[...] = o.astype(BF16)


def normmod_matmul(x, nw, sh, sc, w, *, tm, tn, rows_per_group):
    t, d = x.shape
    n = w.shape[1]
    r = sh.shape[1]
    tpg = rows_per_group // tm
    mod_spec = pl.BlockSpec((1, r, d), lambda i, j: (i // tpg, 0, 0))
    return pl.pallas_call(
        _normmod_mm_kernel,
        grid=(t // tm, n // tn),
        in_specs=[pl.BlockSpec((tm, d), lambda i, j: (i, 0)),
                  pl.BlockSpec((1, d), lambda i, j: (0, 0)),
                  mod_spec, mod_spec,
                  pl.BlockSpec((d, tn), lambda i, j: (0, j))],
        out_specs=[pl.BlockSpec((tm, tn), lambda i, j: (i, j)),
                   pl.BlockSpec((tm, tn), lambda i, j: (i, j))],
        out_shape=[jax.ShapeDtypeStruct((t, n), F32), jax.ShapeDtypeStruct((t, n), BF16)],
        scratch_shapes=[pltpu.VMEM((tm, d), BF16)],
        compiler_params=_cp(("parallel", "arbitrary")),
        name="normmod_matmul",
    )(x, nw, sh, sc, w)


def _normmod_mm_kv_kernel(x_ref, nw_ref, sh_ref, sc_ref, w_ref, kin_ref, vin_ref, ob_ref, k_ref, v_ref, h_ref, *, nseg):
    del kin_ref, vin_ref
    j = pl.program_id(1)

    @pl.when(j == 0)
    def _():
        h_ref[...] = _norm_modulate(x_ref[...], nw_ref[...], sh_ref[0], sc_ref[0]).astype(BF16)

    o = jnp.dot(h_ref[...], w_ref[...], preferred_element_type=F32)
    ob_ref[...] = o.astype(BF16)

    @pl.when((j >= nseg) & (j < 2 * nseg))
    def _():
        k_ref[0] = o

    @pl.when(j >= 2 * nseg)
    def _():
        v_ref[0] = o


def normmod_matmul_kv(x, nw, sh, sc, w, kbuf, vbuf, layer, *, tm, tn, rows_per_group):
    t, d = x.shape
    n = w.shape[1]
    r = sh.shape[1]
    tpg = rows_per_group // tm
    nseg = n // 3 // tn
    mod_spec = pl.BlockSpec((1, r, d), lambda i, j: (i // tpg, 0, 0))
    any_spec = pl.BlockSpec(memory_space=pl.ANY)
    k_spec = pl.BlockSpec((1, tm, tn), lambda i, j: (layer, i, jnp.clip(j - nseg, 0, nseg - 1)))
    v_spec = pl.BlockSpec((1, tm, tn), lambda i, j: (layer, i, jnp.clip(j - 2 * nseg, 0, nseg - 1)))
    return pl.pallas_call(
        functools.partial(_normmod_mm_kv_kernel, nseg=nseg),
        grid=(t // tm, n // tn),
        in_specs=[pl.BlockSpec((tm, d), lambda i, j: (i, 0), pipeline_mode=pl.Buffered(1)),
                  pl.BlockSpec((1, d), lambda i, j: (0, 0)),
                  mod_spec, mod_spec,
                  pl.BlockSpec((d, tn), lambda i, j: (0, j)),
                  any_spec, any_spec],
        out_specs=[pl.BlockSpec((tm, tn), lambda i, j: (i, j)), k_spec, v_spec],
        out_shape=[jax.ShapeDtypeStruct((t, n), BF16),
                   jax.ShapeDtypeStruct(kbuf.shape, F32), jax.ShapeDtypeStruct(vbuf.shape, F32)],
        input_output_aliases={5: 1, 6: 2},
        scratch_shapes=[pltpu.VMEM((tm, d), BF16)],
        compiler_params=_cp(("parallel", "arbitrary")),
        name="normmod_matmul_kv",
    )(x, nw, sh, sc, w, kbuf, vbuf)


def _mm_res_kernel(*refs, n_pairs):
    a_refs = refs[:n_pairs]
    w_refs = refs[n_pairs:2 * n_pairs]
    x_ref, g_ref, o_ref = refs[2 * n_pairs:]
    acc = jnp.dot(a_refs[0][...], w_refs[0][...], preferred_element_type=F32)
    for a_ref, w_ref in zip(a_refs[1:], w_refs[1:]):
        acc = acc + jnp.dot(a_ref[...], w_ref[...], preferred_element_type=F32)
    o_ref[...] = x_ref[...] + g_ref[0] * acc


def matmul_residual(a_list, w_list, x, gate, *, tm, tn, rows_per_group):
    t, n = x.shape
    r = gate.shape[1]
    tpg = rows_per_group // tm
    npairs = len(a_list)
    in_specs = ([pl.BlockSpec((tm, a.shape[1]), lambda i, j: (i, 0)) for a in a_list]
                + [pl.BlockSpec((w.shape[0], tn), lambda i, j: (0, j)) for w in w_list]
                + [pl.BlockSpec((tm, tn), lambda i, j: (i, j)),
                   pl.BlockSpec((1, r, tn), lambda i, j: (i // tpg, 0, j))])
    return pl.pallas_call(
        functools.partial(_mm_res_kernel, n_pairs=npairs),
        grid=(t // tm, n // tn),
        in_specs=in_specs,
        out_specs=pl.BlockSpec((tm, tn), lambda i, j: (i, j)),
        out_shape=jax.ShapeDtypeStruct((t, n), F32),
        compiler_params=_cp(("parallel", "parallel")),
        name="matmul_residual",
    )(*a_list, *w_list, x, gate)


def _rms_kernel(x_ref, w_ref, o_ref):
    x = x_ref[...]
    ms = jnp.mean(x * x, axis=-1, keepdims=True)
    o_ref[...] = x * lax.rsqrt(ms + EPS) * w_ref[...]


def rmsnorm_rows(x, w, *, tm):
    t, d = x.shape
    return pl.pallas_call(
        _rms_kernel,
        grid=(t // tm,),
        in_specs=[pl.BlockSpec((tm, d), lambda i: (i, 0)), pl.BlockSpec((1, d), lambda i: (0, 0))],
        out_specs=pl.BlockSpec((tm, d), lambda i: (i, 0)),
        out_shape=jax.ShapeDtypeStruct((t, d), F32),
        compiler_params=_cp(("parallel",)),
        name="final_rmsnorm",
    )(x, w)


def _top_values_ranked(s, k, want_rank):
    vals = []
    rank = jnp.full(s.shape, float(k), F32) if want_rank else None
    for it in range(k):
        m = jnp.max(s, axis=0, keepdims=True)
        hit = s == m
        if want_rank:
            rank = jnp.where(hit, float(it), rank)
        vals.append(m)
        if it + 1 < k:
            s = jnp.where(hit, -jnp.inf, s)
    return vals, rank


def _split_bf16(a):
    hi = a.astype(BF16)
    lo = (a - hi.astype(F32)).astype(BF16)
    return hi, lo


def _dot3(a_hi, a_lo, b_hi, b_lo, dims):
    f = functools.partial(lax.dot_general, dimension_numbers=dims, preferred_element_type=F32)
    return f(a_hi, b_hi) + (f(a_hi, b_lo) + f(a_lo, b_hi))


NN_DIMS = (((1,), (0,)), ((), ()))


def _peer_route_kernel(x_ref, nw_ref, sh_ref, sc_ref, wqh_ref, wql_ref, kh_ref, kl_ref,
                       h_ref, cnt_ref, r_ref, rank_ref, e2_ref, qt_ref, *, heads, topk):
    h = _norm_modulate(x_ref[...], nw_ref[...], sh_ref[0], sc_ref[0])
    hb, hl = _split_bf16(h)
    h_ref[...] = hb
    qt_ref[...] = _dot3(wqh_ref[0], wql_ref[0], hb, hl, NT_DIMS)
    nk = kh_ref.shape[2]
    tm = hb.shape[0]
    half = topk // 2
    row8 = lax.broadcasted_iota(jnp.int32, (half, tm), 0)

    def head_body(hd, carry):
        s = []
        for c in range(2):
            idx = hd * 2 + c
            qh, ql = _split_bf16(qt_ref[pl.ds(pl.multiple_of(idx * nk, nk), nk), :])
            s.append(_dot3(kh_ref[0, idx], kl_ref[0, idx], qh, ql, NN_DIMS))
        v1, _ = _top_values_ranked(s[0], topk, False)
        v2, rank2 = _top_values_ranked(s[1], topk, True)
        v1s = jnp.concatenate(v1, axis=0)
        v2s = jnp.concatenate(v2, axis=0)
        pieces = [v1[0] + v2s, v1[1] + v2s[0:half]]
        for a in range(2, half):
            pieces.append(jnp.where(row8 < topk // (a + 1), v1[a] + v2s[0:half], -jnp.inf))
        pieces.append(v1s[half:topk] + v2[0])
        cand = jnp.concatenate(pieces, axis=0)
        mx = v1[0] + v2[0]
        z = jnp.zeros_like(mx)
        tau = mx
        for it in range(topk):
            tau = jnp.max(cand, axis=0, keepdims=True)
            z = z + jnp.exp(tau - mx)
            if it + 1 < topk:
                cand = jnp.where(cand == tau, -jnp.inf, cand)
        cnt = jnp.zeros(s[0].shape, F32)
        for b in range(half):
            cnt = cnt + jnp.where(s[0] + v2[b] >= tau, 1.0, 0.0)
        extra = jnp.zeros_like(tau)
        for b in range(half, topk):
            extra = extra + jnp.where(v1[0] + v2[b] >= tau, 1.0, 0.0)
        cnt_ref[hd] = cnt + jnp.where(s[0] == v1[0], extra, 0.0)
        r_ref[hd] = jnp.exp(s[0] - v1[0]) / z
        rank_ref[hd] = rank2.astype(BF16)
        e2_ref[hd] = jnp.exp(s[1] - v2[0]).astype(BF16)
        return carry

    def group_body(hg, carry):
        for k in range(ROUTE_HEADS_PER_TRIP):
            head_body(ROUTE_HEADS_PER_TRIP * hg + k, carry)
        return carry

    lax.fori_loop(0, heads // ROUTE_HEADS_PER_TRIP, group_body, 0)


def peer_route(x, nw, sh, sc, wqt, keys2, layer, *, tm, rows_per_group):
    t, d = x.shape
    r = sh.shape[1]
    nq = wqt[0].shape[1]
    _, nhc, nk, dh = keys2[0].shape
    heads = nhc // 2
    tpg = rows_per_group // tm
    mod_spec = pl.BlockSpec((1, r, d), lambda i: (i // tpg, 0, 0))
    fac_spec = pl.BlockSpec((heads, nk, tm), lambda i: (0, 0, i))
    f32_shape = jax.ShapeDtypeStruct((heads, nk, t), F32)
    bf_shape = jax.ShapeDtypeStruct((heads, nk, t), BF16)
    wq_spec = pl.BlockSpec((1, nq, d), lambda i: (layer, 0, 0), pipeline_mode=pl.Buffered(1))
    key_spec = pl.BlockSpec((1, nhc, nk, dh), lambda i: (layer, 0, 0, 0), pipeline_mode=pl.Buffered(1))
    return pl.pallas_call(
        functools.partial(_peer_route_kernel, heads=heads, topk=PEER_TOPK),
        grid=(t // tm,),
        in_specs=[pl.BlockSpec((tm, d), lambda i: (i, 0)),
                  pl.BlockSpec((1, d), lambda i: (0, 0)),
                  mod_spec, mod_spec, wq_spec, wq_spec, key_spec, key_spec],
        out_specs=[pl.BlockSpec((tm, d), lambda i: (i, 0)), fac_spec, fac_spec, fac_spec, fac_spec],
        out_shape=[jax.ShapeDtypeStruct((t, d), BF16), f32_shape, f32_shape, bf_shape, bf_shape],
        scratch_shapes=[pltpu.VMEM((nq, tm), F32)],
        compiler_params=_cp(("parallel",)),
        name="peer_route",
    )(x, nw, sh, sc, wqt[0], wqt[1], keys2[0], keys2[1])


def _gelu_exact(a):
    return 0.5 * a * (1.0 + lax.erf(a * (1.0 / math.sqrt(2.0))))


def _peer_expert_kernel(h_ref, cnt_ref, r_ref, rank_ref, e2_ref, *rest, heads, ni, sub, nv):
    ns = ni // sub
    u_refs, vt_refs = rest[:ns], rest[ns:ns + nv]
    x_ref, g_ref, o_ref, acc_ref, wa_ref = rest[ns + nv:]
    j = pl.program_id(1)
    nk = rank_ref.shape[1] * rank_ref.shape[2]

    @pl.when(j == 0)
    def _():
        acc_ref[...] = jnp.zeros_like(acc_ref)

    hb = h_ref[...]
    tm = hb.shape[0]
    pk = BF16_ROWS
    for c in range(ns):
        lo, hi = c * sub * nk, (c + 1) * sub * nk
        act = lax.dot_general(u_refs[c][0], hb, NT_DIMS, preferred_element_type=F32)
        act = _gelu_exact(act).astype(BF16)
        gs = []
        for ii in range(c * sub, (c + 1) * sub):
            g = None
            for hd in range(heads):
                cnt = jnp.broadcast_to(cnt_ref[hd, ii:ii + 1, :], (pk, tm)).astype(BF16)[None]
                rr = jnp.broadcast_to(r_ref[hd, ii:ii + 1, :], (pk, tm)).astype(BF16)[None]
                contrib = jnp.where(rank_ref[hd] < cnt, e2_ref[hd] * rr, jnp.zeros((), BF16))
                g = contrib if g is None else g + contrib
            gs.append(g.reshape(nk, tm))
        wa_ref[lo:hi, :] = jnp.concatenate(gs, axis=0) * act
    wa = wa_ref[...]
    dr = acc_ref.shape[0] // nv
    for c in range(nv):
        acc_ref[c * dr:(c + 1) * dr, :] += jnp.dot(vt_refs[c][0], wa, preferred_element_type=F32)

    @pl.when(j == pl.num_programs(1) - 1)
    def _():
        o_ref[...] = x_ref[...] + g_ref[0] * acc_ref[...].T


def peer_experts(h, cnt, rfac, rank, e2, u, vt, layer, x, gate, *, tm, ni, sub, rows_per_group):
    t, d = x.shape
    heads, nk, _ = cnt.shape
    r = gate.shape[1]
    tpg = rows_per_group // tm
    te = ni * nk
    row_spec = pl.BlockSpec((heads, ni, tm), lambda i, j: (0, j, i))
    full_spec = pl.BlockSpec((heads, nk // BF16_ROWS, BF16_ROWS, tm), lambda i, j: (0, 0, 0, i))
    rank = rank.reshape(heads, nk // BF16_ROWS, BF16_ROWS, t)
    e2 = e2.reshape(heads, nk // BF16_ROWS, BF16_ROWS, t)
    ns = ni // sub
    nv = ns if tm <= LANE else 1
    u_specs = [pl.BlockSpec((1, te // ns, d), lambda i, j, c=c: (layer, j * ns + c, 0)) for c in range(ns)]
    vt_specs = [pl.BlockSpec((1, d // nv, te), lambda i, j, c=c: (layer, c, j)) for c in range(nv)]
    return pl.pallas_call(
        functools.partial(_peer_expert_kernel, heads=heads, ni=ni, sub=sub, nv=nv),
        grid=(t // tm, nk // ni),
        in_specs=[pl.BlockSpec((tm, d), lambda i, j: (i, 0)),
                  row_spec, row_spec, full_spec, full_spec, *u_specs, *vt_specs,
                  pl.BlockSpec((tm, d), lambda i, j: (i, 0), pipeline_mode=pl.Buffered(1)),
                  pl.BlockSpec((1, r, d), lambda i, j: (i // tpg, 0, 0))],
        out_specs=pl.BlockSpec((tm, d), lambda i, j: (i, 0)),
        out_shape=jax.ShapeDtypeStruct((t, d), F32),
        scratch_shapes=[pltpu.VMEM((d, tm), F32), pltpu.VMEM((te, tm), BF16)],
        compiler_params=_cp(("parallel", "arbitrary")),
        name="peer_experts",
    )(h, cnt, rfac, rank, e2, *([u] * ns), *([vt] * nv), x, gate)


def _dilated_mult_table(tq, tk, n_off):
    o = jnp.arange(n_off, dtype=jnp.int32)[:, None, None]
    r = jnp.arange(tq, dtype=jnp.int32)[None, :, None]
    c = jnp.arange(tk, dtype=jnp.int32)[None, None, :]
    d = o * tk + r - c
    mult = jnp.zeros((n_off, tq, tk), F32)
    for win, dil in DILATED_PATTERNS:
        steps = win // dil
        ok = (d >= 0) & (d % dil == 0) & (d <= steps * dil)
        mult = mult + ok.astype(F32)
    return mult


def _dil_attn_kernel(q_ref, k_ref, v_ref, mult_ref, o_ref, m_ref, l_ref, acc_ref, *, t, n_off, scale):
    qi = pl.program_id(2)
    q = (q_ref[...].astype(F32) * scale).astype(BF16)
    m_ref[...] = jnp.full_like(m_ref, NEG_BIG)
    l_ref[...] = jnp.zeros_like(l_ref)
    acc_ref[...] = jnp.zeros_like(acc_ref)

    def body(kb, carry):
        off = qi - kb
        start = pl.multiple_of(kb * t, t)
        k = k_ref[pl.ds(start, t), :]
        v = v_ref[pl.ds(start, t), :]
        s = lax.dot_general(q, k, NT_DIMS, preferred_element_type=F32)
        mult = mult_ref[off]
        s = jnp.where(mult > 0.0, s, NEG_BIG)
        m_old = m_ref[...]
        m_new = jnp.maximum(m_old, jnp.max(s, axis=-1, keepdims=True))
        alpha = jnp.exp(m_old - m_new)
        p = jnp.exp(s - jnp.tile(m_new, (1, t // LANE))) * mult
        l_ref[...] = alpha * l_ref[...] + jnp.sum(p, axis=-1, keepdims=True)
        acc_ref[...] = alpha * acc_ref[...] + jnp.dot(p.astype(BF16), v, preferred_element_type=F32)
        m_ref[...] = m_new
        return carry

    lax.fori_loop(jnp.maximum(qi - (n_off - 1), 0), qi + 1, body, 0)
    o_ref[...] = (acc_ref[...] / l_ref[...]).astype(o_ref.dtype)


def dilated_attention_prompt(proj_bf, *, batch, seq, heads, t):
    win_max = min(max(w for w, _ in DILATED_PATTERNS), seq)
    n_off = min(-(-win_max // t) + 1, seq // t)
    mult = _dilated_mult_table(t, t, n_off)
    nq = seq // t
    kern = functools.partial(_dil_attn_kernel, t=t, n_off=n_off, scale=1.0 / math.sqrt(HD_A))
    return pl.pallas_call(
        kern,
        grid=(batch, heads, nq),
        in_specs=[pl.BlockSpec((t, HD_A), lambda b, h, i: (b * nq + i, h)),
                  pl.BlockSpec((seq, HD_A), lambda b, h, i: (b, heads + h)),
                  pl.BlockSpec((seq, HD_A), lambda b, h, i: (b, 2 * heads + h)),
                  pl.BlockSpec((n_off, t, t), lambda b, h, i: (0, 0, 0))],
        out_specs=pl.BlockSpec((t, HD_A), lambda b, h, i: (b * nq + i, h)),
        out_shape=jax.ShapeDtypeStruct((batch * seq, heads * HD_A), BF16),
        scratch_shapes=[pltpu.VMEM((t, LANE), F32), pltpu.VMEM((t, LANE), F32), pltpu.VMEM((t, HD_A), F32)],
        compiler_params=_cp(("parallel", "parallel", "arbitrary")),
        name="dilated_attention",
    )(proj_bf, proj_bf, proj_bf, mult)


def _pair_cols(q, p, lane_lo):
    a = q[:, 2 * p:2 * p + 1]
    b = q[:, 2 * p + 1:2 * p + 2]
    return jnp.where(lane_lo, a, b)


def _ssd_kernel(z_ref, xbc_ref, dt_ref, cw_ref, cb_ref, dtb_ref, alog_ref, dsk_ref, nw_ref,
                y_ref, st_ref, xprev_ref, s_ref, *, d_inner, n_pairs, conv_w):
    c = pl.program_id(1)
    q = xbc_ref.shape[0]
    pre = SUBLANE

    @pl.when(c == 0)
    def _():
        xprev_ref[0:pre, :] = jnp.zeros((pre, xprev_ref.shape[1]), F32)
        s_ref[...] = jnp.zeros_like(s_ref)

    x_in = xbc_ref[...]
    xprev_ref[pre:pre + q, :] = x_in
    conv = cb_ref[...]
    for kk in range(conv_w):
        conv = conv + xprev_ref[pl.ds(pre - (conv_w - 1) + kk, q), :] * cw_ref[kk:kk + 1, :]
    xprev_ref[0:pre, :] = x_in[q - pre:q, :]
    act = _silu(conv)
    gn = SSM_GROUPS * D_STATE
    xs = act[:, :d_inner]
    bm = act[:, d_inner:d_inner + gn]
    cm = act[:, d_inner + gn:d_inner + 2 * gn]

    dt = _softplus(dt_ref[...] + dtb_ref[...])
    a = -jnp.exp(alog_ref[...])
    dta = dt * a
    row = lax.broadcasted_iota(jnp.int32, (q, q), 0)
    col = lax.broadcasted_iota(jnp.int32, (q, q), 1)
    causal = row >= col
    cum = jnp.dot(causal.astype(F32), dta, preferred_element_type=F32, precision=lax.Precision.HIGHEST)
    cum_t = cum.T
    dt_t = dt.T
    cum_end = cum[q - 1:q, :]
    wdec = jnp.exp(cum_end - cum) * dt
    ecum = jnp.exp(cum)
    cdec = jnp.exp(cum_end)
    lane_lo = lax.broadcasted_iota(jnp.int32, (q, LANE), 1) < SSM_HEADDIM
    lane_lo1 = lane_lo[0:1, :]
    pairs_per_group = (n_pairs * 2 // SSM_GROUPS) // 2

    ys = []
    for g in range(SSM_GROUPS):
        bg = bm[:, g * D_STATE:(g + 1) * D_STATE]
        cg = cm[:, g * D_STATE:(g + 1) * D_STATE].astype(BF16)
        cb = lax.dot_general(cg, bg.astype(BF16), NT_DIMS, preferred_element_type=F32)
        bg_t = bg.T.astype(BF16)
        for pp in range(pairs_per_group):
            p = g * pairs_per_group + pp
            xp = xs[:, p * LANE:(p + 1) * LANE]
            xpb = xp.astype(BF16)
            y_intra = None
            for hh in range(2):
                h = 2 * p + hh
                seg = cum[:, h:h + 1] - cum_t[h:h + 1, :]
                dec = jnp.exp(jnp.where(causal, seg, -jnp.inf))
                mm = (cb * dec * dt_t[h:h + 1, :]).astype(BF16)
                yh = jnp.dot(mm, xpb, preferred_element_type=F32)
                y_intra = yh if y_intra is None else jnp.where(lane_lo, y_intra, yh)
            sp = s_ref[p]
            y_inter = jnp.dot(cg, sp.astype(BF16), preferred_element_type=F32) * _pair_cols(ecum, p, lane_lo)
            ys.append(y_intra + y_inter + dsk_ref[:, p * LANE:(p + 1) * LANE] * xp)
            xw = (xp * _pair_cols(wdec, p, lane_lo)).astype(BF16)
            s_ref[p] = (_pair_cols(cdec, p, lane_lo1) * sp
                        + jnp.dot(bg_t, xw, preferred_element_type=F32))
    y = jnp.concatenate(ys, axis=1)
    y = y * _silu(z_ref[...])
    gw = d_inner // SSM_GROUPS
    outs = []
    for g in range(SSM_GROUPS):
        yg = y[:, g * gw:(g + 1) * gw]
        outs.append(yg * lax.rsqrt(jnp.mean(yg * yg, axis=-1, keepdims=True) + EPS))
    y_ref[...] = (jnp.concatenate(outs, axis=1) * nw_ref[...]).astype(y_ref.dtype)

    @pl.when(c == pl.num_programs(1) - 1)
    def _():
        st_ref[0] = s_ref[...]


def ssd_prompt(proj, conv_w, conv_b, dtb_pad, alog_pad, dskip_lane, norm_w, *, batch, seq, d_inner, col0):
    q = SSD_CHUNK
    nc = seq // q
    conv_dim = conv_w.shape[1]
    cw = conv_w.shape[0]
    n_pairs = d_inner // (2 * SSM_HEADDIM)
    zb = col0 // d_inner
    xb = (col0 + d_inner) // conv_dim
    db = (col0 + d_inner + conv_dim) // LANE
    kern = functools.partial(_ssd_kernel, d_inner=d_inner, n_pairs=n_pairs, conv_w=cw)
    const = lambda b, c: (0, 0)
    return pl.pallas_call(
        kern,
        grid=(batch, nc),
        in_specs=[pl.BlockSpec((q, d_inner), lambda b, c: (b * nc + c, zb)),
                  pl.BlockSpec((q, conv_dim), lambda b, c: (b * nc + c, xb)),
                  pl.BlockSpec((q, LANE), lambda b, c: (b * nc + c, db)),
                  pl.BlockSpec((cw, conv_dim), const),
                  pl.BlockSpec((1, conv_dim), const),
                  pl.BlockSpec((1, LANE), const),
                  pl.BlockSpec((1, LANE), const),
                  pl.BlockSpec((1, d_inner), const),
                  pl.BlockSpec((1, d_inner), const)],
        out_specs=[pl.BlockSpec((q, d_inner), lambda b, c: (b * nc + c, 0)),
                   pl.BlockSpec((1, n_pairs, D_STATE, LANE), lambda b, c: (b, 0, 0, 0))],
        out_shape=[jax.ShapeDtypeStruct((batch * seq, d_inner), BF16),
                   jax.ShapeDtypeStruct((batch, n_pairs, D_STATE, LANE), F32)],
        scratch_shapes=[pltpu.VMEM((SUBLANE + q, conv_dim), F32),
                        pltpu.VMEM((n_pairs, D_STATE, LANE), F32)],
        compiler_params=_cp(("parallel", "arbitrary")),
        name="ssd_prompt",
    )(proj, proj, proj, conv_w, conv_b, dtb_pad, alog_pad, dskip_lane, norm_w)


def _state_from_pairs(st):
    b, npair, n, _ = st.shape
    return st.reshape(b, npair, n, 2, SSM_HEADDIM).transpose(0, 1, 3, 4, 2).reshape(b, 2 * npair, SSM_HEADDIM, n)


def _diff_lambda(lamp, lam_init):
    l1 = jnp.sum(lamp[0:1] * lamp[1:2], axis=-1, keepdims=True)
    l2 = jnp.sum(lamp[2:3] * lamp[3:4], axis=-1, keepdims=True)
    return jnp.exp(l1) - jnp.exp(l2) + lam_init


def _diff_finish(acc0, l0, acc1, l1, lam, sw, lam_init):
    o = acc0 / l0 - lam * (acc1 / l1)
    o = o * lax.rsqrt(jnp.mean(o * o, axis=-1, keepdims=True) + EPS) * sw
    return o * (1.0 - lam_init)


def _diff_attn_kernel(lamp_ref, sw_ref, q_ref, k_ref, v_ref, o_ref, m_ref, l_ref, acc_ref, *, t, scale, lam_init):
    qi = pl.program_id(2)
    q = (q_ref[...].astype(F32) * scale).astype(BF16)
    m_ref[...] = jnp.full_like(m_ref, NEG_BIG)
    l_ref[...] = jnp.zeros_like(l_ref)
    acc_ref[...] = jnp.zeros_like(acc_ref)
    row = lax.broadcasted_iota(jnp.int32, (t, t), 0)
    col = lax.broadcasted_iota(jnp.int32, (t, t), 1)

    def step(kb, masked):
        start = pl.multiple_of(kb * t, t)
        k = k_ref[pl.ds(start, t), :]
        v = v_ref[pl.ds(start, t), :]
        for m in range(2):
            s = lax.dot_general(q[:, m * HD_C:(m + 1) * HD_C], k[:, m * HD_C:(m + 1) * HD_C], NT_DIMS,
                                preferred_element_type=F32)
            if masked:
                s = jnp.where(row >= col, s, NEG_BIG)
            m_old = m_ref[m]
            m_new = jnp.maximum(m_old, jnp.max(s, axis=-1, keepdims=True))
            alpha = jnp.exp(m_old - m_new)
            p = jnp.exp(s - jnp.tile(m_new, (1, t // LANE)))
            l_ref[m] = alpha * l_ref[m] + jnp.sum(p, axis=-1, keepdims=True)
            acc_ref[m] = (jnp.tile(alpha, (1, v.shape[1] // LANE)) * acc_ref[m]
                          + jnp.dot(p.astype(BF16), v, preferred_element_type=F32))
            m_ref[m] = m_new

    def body(kb, carry):
        step(kb, False)
        return carry

    lax.fori_loop(0, qi, body, 0)
    step(qi, True)
    lam = _diff_lambda(lamp_ref[...], lam_init)
    o_ref[...] = _diff_finish(acc_ref[0], l_ref[0][:, 0:1], acc_ref[1], l_ref[1][:, 0:1], lam, sw_ref[...],
                              lam_init).astype(o_ref.dtype)


def diff_attention_prompt(proj_bf, lamp, subln_w, *, batch, seq, heads, t, lam_init):
    hw = 2 * HD_C
    nq = seq // t
    kern = functools.partial(_diff_attn_kernel, t=t, scale=1.0 / math.sqrt(HD_C), lam_init=lam_init)
    return pl.pallas_call(
        kern,
        grid=(batch, heads, nq),
        in_specs=[pl.BlockSpec((4, HD_C), lambda b, h, i: (0, 0)),
                  pl.BlockSpec((1, hw), lambda b, h, i: (0, 0)),
                  pl.BlockSpec((t, hw), lambda b, h, i: (b * nq + i, h)),
                  pl.BlockSpec((seq, hw), lambda b, h, i: (b, heads + h)),
                  pl.BlockSpec((seq, hw), lambda b, h, i: (b, 2 * heads + h))],
        out_specs=pl.BlockSpec((t, hw), lambda b, h, i: (b * nq + i, h)),
        out_shape=jax.ShapeDtypeStruct((batch * seq, heads * hw), BF16),
        scratch_shapes=[pltpu.VMEM((2, t, LANE), F32), pltpu.VMEM((2, t, LANE), F32), pltpu.VMEM((2, t, hw), F32)],
        compiler_params=_cp(("parallel", "parallel", "arbitrary")),
        name="diff_attention",
    )(lamp, subln_w, proj_bf, proj_bf, proj_bf)


def _diff_sample_kernel(pt_ref, q_ref, kn_ref, vn_ref, lamp_ref, sw_ref, *rest, pp, scale, lam_init):
    del pt_ref
    k_refs = rest[:pp]
    v_refs = rest[pp:2 * pp]
    o_ref, m_ref, l_ref, acc_ref = rest[2 * pp:]
    pg = pl.program_id(1)
    q = q_ref[0] * scale

    @pl.when(pg == 0)
    def _():
        prod = q * kn_ref[0]
        for m in range(2):
            m_ref[m] = jnp.sum(prod[:, m * HD_C:(m + 1) * HD_C], axis=-1, keepdims=True)
            l_ref[m] = jnp.ones_like(l_ref[m])
            acc_ref[m] = vn_ref[0]

    for i in range(pp):
        kk = k_refs[i][0, 0]
        vv = v_refs[i][0, 0]
        prod = kk * q[None]
        for m in range(2):
            s = jnp.sum(prod[:, :, m * HD_C:(m + 1) * HD_C], axis=-1, keepdims=True)
            m_old = m_ref[m]
            m_new = jnp.maximum(m_old, jnp.max(s, axis=0))
            alpha = jnp.exp(m_old - m_new)
            p = jnp.exp(s - m_new[None])
            l_ref[m] = alpha * l_ref[m] + jnp.sum(p, axis=0)
            acc_ref[m] = alpha * acc_ref[m] + jnp.sum(p * vv, axis=0)
            m_ref[m] = m_new

    @pl.when(pg == pl.num_programs(1) - 1)
    def _():
        lam = _diff_lambda(lamp_ref[...], lam_init)
        o_ref[0] = _diff_finish(acc_ref[0], l_ref[0], acc_ref[1], l_ref[1], lam, sw_ref[...], lam_init)


def diff_attention_sample(page_table, q3, kn3, vn3, lamp, subln_w, cache_k, cache_v, *, layer, pp, lam_init):
    bsz, heads, hw = q3.shape
    n_pages = page_table.shape[1]
    page = cache_k.shape[2]
    row_spec = pl.BlockSpec((1, heads, hw), lambda b, p, pt: (b, 0, 0))

    def page_spec(i):
        return pl.BlockSpec((1, 1, page, heads, hw), lambda b, p, pt: (layer, pt[b, p * pp + i], 0, 0, 0))

    kern = functools.partial(_diff_sample_kernel, pp=pp, scale=1.0 / math.sqrt(HD_C), lam_init=lam_init)
    grid_spec = pltpu.PrefetchScalarGridSpec(
        num_scalar_prefetch=1,
        grid=(bsz, n_pages // pp),
        in_specs=[row_spec, row_spec, row_spec,
                  pl.BlockSpec((4, HD_C), lambda b, p, pt: (0, 0)),
                  pl.BlockSpec((1, hw), lambda b, p, pt: (0, 0))]
                 + [page_spec(i) for i in range(pp)] + [page_spec(i) for i in range(pp)],
        out_specs=row_spec,
        scratch_shapes=[pltpu.VMEM((2, heads, 1), F32), pltpu.VMEM((2, heads, 1), F32),
                        pltpu.VMEM((2, heads, hw), F32)],
    )
    return pl.pallas_call(
        kern,
        grid_spec=grid_spec,
        out_shape=jax.ShapeDtypeStruct((bsz, heads, hw), F32),
        compiler_params=_cp(("parallel", "arbitrary")),
        name="diff_attention_sample",
    )(page_table, q3, kn3, vn3, lamp, subln_w, *([cache_k] * pp), *([cache_v] * pp))


def _win_sample_kernel(q_ref, kn_ref, vn_ref, ck_ref, cv_ref, o_ref, m_ref, l_ref, acc_ref, *, tr, wb, scale):
    r = pl.program_id(1)
    q = q_ref[0] * scale

    @pl.when(r == 0)
    def _():
        m_ref[...] = jnp.sum(q * kn_ref[0], axis=-1, keepdims=True)
        l_ref[...] = jnp.full_like(l_ref, float(len(DILATED_PATTERNS)))
        acc_ref[...] = float(len(DILATED_PATTERNS)) * vn_ref[0]

    kk = ck_ref[0, 0]
    s = jnp.sum(kk * q[None], axis=-1, keepdims=True)
    j = r * tr + lax.broadcasted_iota(jnp.int32, s.shape, 0)
    d = wb - j
    mult = jnp.zeros(s.shape, F32)
    for win, dil in DILATED_PATTERNS:
        steps = win // dil
        mult = mult + ((d % dil == 0) & (d <= steps * dil)).astype(F32)
    s = jnp.where(mult > 0.0, s, NEG_BIG)
    m_old = m_ref[...]
    m_new = jnp.maximum(m_old, jnp.max(s, axis=0))
    alpha = jnp.exp(m_old - m_new)
    p = jnp.exp(s - m_new[None]) * mult
    l_ref[...] = alpha * l_ref[...] + jnp.sum(p, axis=0)
    acc_ref[...] = alpha * acc_ref[...] + jnp.sum(p * cv_ref[0, 0], axis=0)
    m_ref[...] = m_new

    @pl.when(r == pl.num_programs(1) - 1)
    def _():
        o_ref[0] = acc_ref[...] / l_ref[...]


def dilated_attention_sample(q3, kn3, vn3, cache_k, cache_v, layer, *, tr):
    bsz, heads, hd = q3.shape
    wb = cache_k.shape[2]
    row_spec = pl.BlockSpec((1, heads, hd), lambda b, r: (b, 0, 0))
    c_spec = pl.BlockSpec((1, 1, tr, heads, hd), lambda b, r: (layer, b, r, 0, 0))
    kern = functools.partial(_win_sample_kernel, tr=tr, wb=wb, scale=1.0 / math.sqrt(hd))
    return pl.pallas_call(
        kern,
        grid=(bsz, wb // tr),
        in_specs=[row_spec, row_spec, row_spec, c_spec, c_spec],
        out_specs=row_spec,
        out_shape=jax.ShapeDtypeStruct((bsz, heads, hd), F32),
        scratch_shapes=[pltpu.VMEM((heads, 1), F32), pltpu.VMEM((heads, 1), F32), pltpu.VMEM((heads, hd), F32)],
        compiler_params=_cp(("parallel", "arbitrary")),
        name="dilated_attention_sample",
    )(q3, kn3, vn3, cache_k, cache_v)


def _lane_to_col(v):
    n = v.shape[1] // LANE
    return jnp.concatenate([jnp.broadcast_to(v[:, k * LANE:(k + 1) * LANE], (LANE, LANE)).T for k in range(n)], axis=0)


def _col_to_lane(c):
    n = c.shape[0] // LANE
    return jnp.concatenate([jnp.broadcast_to(c[k * LANE:(k + 1) * LANE], (LANE, LANE)).T[0:1] for k in range(n)], axis=1)


def _ssd_step_kernel(z_ref, xbc_ref, dt_ref, cs_ref, st_ref, cw_ref, cb_ref, dtb_ref, alog_ref, dsk_ref, nw_ref,
                     e_ref, y_ref, cso_ref, sto_ref, *, d_inner):
    x = xbc_ref[0]
    cs = cs_ref[0]
    ncs = cs.shape[0]
    conv = cb_ref[...] + x * cw_ref[ncs:ncs + 1, :]
    for kk in range(ncs):
        conv = conv + cs[kk:kk + 1, :] * cw_ref[kk:kk + 1, :]
    cso_ref[0, 0:ncs - 1, :] = cs[1:ncs, :]
    cso_ref[0, ncs - 1:ncs, :] = x
    act = _silu(conv)
    gn = SSM_GROUPS * D_STATE
    xs = act[:, :d_inner]
    bm = act[:, d_inner:d_inner + gn]
    cm = act[:, d_inner + gn:d_inner + 2 * gn]
    dt = _softplus(dt_ref[0] + dtb_ref[...])
    dt_lane = jnp.dot(jnp.broadcast_to(dt, (SUBLANE, LANE)), e_ref[...], preferred_element_type=F32,
                      precision=lax.Precision.HIGHEST)[0:1]
    dec_lane = jnp.exp(dt_lane * (-jnp.exp(alog_ref[...])))
    dec_col = _lane_to_col(dec_lane)
    dtx_col = _lane_to_col(dt_lane * xs)
    rows_g = d_inner // SSM_GROUPS
    b_rows = jnp.concatenate([jnp.broadcast_to(bm[:, g * D_STATE:(g + 1) * D_STATE], (rows_g, D_STATE))
                              for g in range(SSM_GROUPS)], axis=0)
    c_rows = jnp.concatenate([jnp.broadcast_to(cm[:, g * D_STATE:(g + 1) * D_STATE], (rows_g, D_STATE))
                              for g in range(SSM_GROUPS)], axis=0)
    st = st_ref[0].reshape(d_inner, D_STATE)
    st_new = dec_col * st + dtx_col * b_rows
    sto_ref[0] = st_new.reshape(sto_ref.shape[1:])
    y = _col_to_lane(jnp.sum(st_new * c_rows, axis=-1, keepdims=True))
    y = y + dsk_ref[...] * xs
    y = y * _silu(z_ref[0])
    gw = d_inner // SSM_GROUPS
    outs = []
    for g in range(SSM_GROUPS):
        yg = y[:, g * gw:(g + 1) * gw]
        outs.append(yg * lax.rsqrt(jnp.mean(yg * yg, axis=-1, keepdims=True) + EPS))
    y_ref[0] = jnp.concatenate(outs, axis=1) * nw_ref[...]


def ssd_step(z3, xbc3, dt3, conv_state, ssm_state, conv_w, conv_b, dtb_pad, alog_lane, dskip_lane, norm_w, expand):
    bsz, _, d_inner = z3.shape
    conv_dim = xbc3.shape[2]
    ncs = conv_state.shape[1]
    hb, hp, n = ssm_state.shape[1:]
    const = lambda b: (0, 0)
    kern = functools.partial(_ssd_step_kernel, d_inner=d_inner)
    return pl.pallas_call(
        kern,
        grid=(bsz,),
        in_specs=[pl.BlockSpec((1, 1, d_inner), lambda b: (b, 0, 0)),
                  pl.BlockSpec((1, 1, conv_dim), lambda b: (b, 0, 0)),
                  pl.BlockSpec((1, 1, LANE), lambda b: (b, 0, 0)),
                  pl.BlockSpec((1, ncs, conv_dim), lambda b: (b, 0, 0)),
                  pl.BlockSpec((1, hb, hp, n), lambda b: (b, 0, 0, 0)),
                  pl.BlockSpec((ncs + 1, conv_dim), const),
                  pl.BlockSpec((1, conv_dim), const),
                  pl.BlockSpec((1, LANE), const),
                  pl.BlockSpec((1, d_inner), const),
                  pl.BlockSpec((1, d_inner), const),
                  pl.BlockSpec((1, d_inner), const),
                  pl.BlockSpec((LANE, d_inner), const)],
        out_specs=[pl.BlockSpec((1, 1, d_inner), lambda b: (b, 0, 0)),
                   pl.BlockSpec((1, ncs, conv_dim), lambda b: (b, 0, 0)),
                   pl.BlockSpec((1, hb, hp, n), lambda b: (b, 0, 0, 0))],
        out_shape=[jax.ShapeDtypeStruct((bsz, 1, d_inner), F32),
                   jax.ShapeDtypeStruct(conv_state.shape, F32),
                   jax.ShapeDtypeStruct(ssm_state.shape, F32)],
        compiler_params=_cp(("parallel",)),
        name="ssd_step",
    )(z3, xbc3, dt3, conv_state, ssm_state, conv_w, conv_b, dtb_pad, alog_lane, dskip_lane, norm_w, expand)


PROMPT_TM = 512
NORMMOD_TM = 1024
ROUTE_TM = 256
EXPERT_NI = 8
EXPERT_SUB = 2
ATTN_T = 512
SAMPLE_PAD = LANE
WIN_SAMPLE_ROWS = 512
PAGES_PER_STEP = 8


def _pad_cols(w, n):
    return jnp.pad(w, ((0, 0), (0, n - w.shape[1])))


def _pad_rows(a, n):
    return jnp.pad(a, ((0, n - a.shape[0]),) + ((0, 0),) * (a.ndim - 1))


def _round_up(n, m):
    return -(-n // m) * m


def kernel(x_prompt, x_sample, c_prompt, c_sample, cache_win_k, cache_win_v, state_conv, state_ssm,
           cache_diff_k, cache_diff_v, page_table, norm1_w, norm2_w, w_ada, b_ada, w_in_even, w_out_even,
           conv_w, conv_b, dt_bias, a_log, d_skip, ssm_norm_w, w_in_odd, w_out_odd, lambda_q1, lambda_k1,
           lambda_q2, lambda_k2, subln_w, peer_wq, peer_keys, peer_u, peer_v, final_norm_w):
    bsz, seq, d = x_prompt.shape
    sb, st, _ = x_sample.shape
    assert st == 1, "one new token per sample sequence"
    depth = w_ada.shape[0]
    h_a = cache_win_k.shape[3]
    a_width = h_a * HD_A
    d_inner = ssm_norm_w.shape[1]
    conv_dim = conv_w.shape[2]
    h_b = dt_bias.shape[1]
    h_c = cache_diff_k.shape[3]
    c_width = h_c * 2 * HD_C
    wb_p = min(max(w for w, _ in DILATED_PATTERNS), seq)
    ncs = conv_w.shape[1] - 1

    xp = x_prompt.reshape(bsz * seq, d)
    xs = x_sample.reshape(sb, d)

    n_c = _round_up(bsz + sb, SUBLANE)
    c_all = _pad_rows(jnp.concatenate([c_prompt, c_sample], axis=0), n_c)
    mod = adaln_all(c_all, w_ada, b_ada)

    expand = (jnp.arange(LANE)[:, None] == (jnp.arange(d_inner) // SSM_HEADDIM)[None, :]).astype(F32)

    wq_all = _split_bf16(jnp.swapaxes(peer_wq, 1, 2))
    nhc = peer_keys.shape[1] * peer_keys.shape[2]
    keys_all = _split_bf16(peer_keys.reshape(depth, nhc, peer_keys.shape[3], peer_keys.shape[4]))
    u_all = peer_u.astype(BF16)
    vt_all = jnp.swapaxes(peer_v, 1, 2).astype(BF16)

    outs = {k: [] for k in ("pwk", "pwv", "pcv", "pss", "swk", "swv", "scv", "sss", "sdk", "sdv")}
    pdk_buf = jnp.zeros((depth // 2, bsz * seq, c_width), F32)
    pdv_buf = jnp.zeros((depth // 2, bsz * seq, c_width), F32)
    for l in range(depth):
        parts = [mod[l, :, i * d:(i + 1) * d] for i in range(6)]
        pm = [p[:bsz][:, None, :] for p in parts]
        sm = [p[bsz:bsz + sb][None] for p in parts]
        n1 = norm1_w[l][None]
        n2 = norm2_w[l][None]
        if l % 2 == 0:
            e = l // 2
            even_in = w_in_even.shape[2]
            n_pad = _round_up(even_in, 7 * LANE)
            w_in = _pad_cols(w_in_even[e], n_pad).astype(BF16)
            tn = n_pad // 7
            wo = w_out_even[e].astype(BF16)
            dtb_pad = _pad_cols(dt_bias[e][None], LANE)
            alog_pad = _pad_cols(a_log[e][None], LANE)
            alog_lane = jnp.repeat(a_log[e], SSM_HEADDIM)[None]
            dsk_lane = jnp.repeat(d_skip[e], SSM_HEADDIM)[None]
            cw, cb, nw = conv_w[e], conv_b[e][None], ssm_norm_w[e][None]
            col_z = 3 * a_width

            proj, proj_bf = normmod_matmul(xp, n1, pm[0], pm[1], w_in, tm=min(NORMMOD_TM, seq), tn=tn,
                                           rows_per_group=seq)
            ya = dilated_attention_prompt(proj_bf, batch=bsz, seq=seq, heads=h_a, t=ATTN_T)
            yb, st_pairs = ssd_prompt(proj, cw, cb, dtb_pad, alog_pad, dsk_lane, nw,
                                      batch=bsz, seq=seq, d_inner=d_inner, col0=col_z)
            xp = matmul_residual([ya, yb], [wo[:a_width], wo[a_width:]], xp, pm[2],
                                 tm=PROMPT_TM, tn=d, rows_per_group=seq)
            proj3 = proj.reshape(bsz, seq, n_pad)
            outs["pwk"].append(proj3[:, seq - wb_p:, a_width:2 * a_width].reshape(bsz, wb_p, h_a, HD_A))
            outs["pwv"].append(proj3[:, seq - wb_p:, 2 * a_width:3 * a_width].reshape(bsz, wb_p, h_a, HD_A))
            outs["pcv"].append(proj3[:, seq - ncs:, col_z + d_inner:col_z + d_inner + conv_dim])
            outs["pss"].append(_state_from_pairs(st_pairs))

            sproj, _ = normmod_matmul(xs, n1, sm[0], sm[1], w_in, tm=sb, tn=tn, rows_per_group=sb)
            q3 = sproj[:, :a_width].reshape(sb, h_a, HD_A)
            k3 = sproj[:, a_width:2 * a_width].reshape(sb, h_a, HD_A)
            v3 = sproj[:, 2 * a_width:3 * a_width].reshape(sb, h_a, HD_A)
            ya_s = dilated_attention_sample(q3, k3, v3, cache_win_k, cache_win_v, e,
                                            tr=min(WIN_SAMPLE_ROWS, cache_win_k.shape[2]))
            z3 = sproj[:, col_z:col_z + d_inner][:, None, :]
            xbc3 = sproj[:, col_z + d_inner:col_z + d_inner + conv_dim][:, None, :]
            dt3 = sproj[:, col_z + d_inner + conv_dim:col_z + d_inner + conv_dim + LANE][:, None, :]
            yb_s, cs_new, ss_new = ssd_step(z3, xbc3, dt3, state_conv[e], state_ssm[e], cw, cb, dtb_pad,
                                            alog_lane, dsk_lane, nw, expand)
            xs = matmul_residual([ya_s.reshape(sb, a_width).astype(BF16), yb_s.reshape(sb, d_inner).astype(BF16)],
                                 [wo[:a_width], wo[a_width:]], xs, sm[2], tm=sb, tn=1024, rows_per_group=sb)
            outs["swk"].append(k3[:, None])
            outs["swv"].append(v3[:, None])
            outs["scv"].append(cs_new)
            outs["sss"].append(ss_new)
        else:
            o_ = l // 2
            lam_init = 0.8 - 0.6 * math.exp(-0.3 * l)
            w_in = w_in_odd[o_].astype(BF16)
            wo = w_out_odd[o_].astype(BF16)
            lamp = jnp.stack([lambda_q1[o_], lambda_k1[o_], lambda_q2[o_], lambda_k2[o_]])
            sw = subln_w[o_][None]

            proj_bf, pdk_buf, pdv_buf = normmod_matmul_kv(xp, n1, pm[0], pm[1], w_in, pdk_buf, pdv_buf, o_,
                                                          tm=min(NORMMOD_TM, seq), tn=1024, rows_per_group=seq)
            o = diff_attention_prompt(proj_bf, lamp, sw, batch=bsz, seq=seq, heads=h_c, t=ATTN_T, lam_init=lam_init)
            xp = matmul_residual([o], [wo], xp, pm[2], tm=PROMPT_TM, tn=d, rows_per_group=seq)

            sproj, _ = normmod_matmul(xs, n1, sm[0], sm[1], w_in, tm=sb, tn=1024, rows_per_group=sb)
            q3 = sproj[:, :c_width].reshape(sb, h_c, 2 * HD_C)
            k3 = sproj[:, c_width:2 * c_width].reshape(sb, h_c, 2 * HD_C)
            v3 = sproj[:, 2 * c_width:3 * c_width].reshape(sb, h_c, 2 * HD_C)
            o_s = diff_attention_sample(page_table, q3, k3, v3, lamp, sw, cache_diff_k, cache_diff_v,
                                        layer=o_, pp=PAGES_PER_STEP, lam_init=lam_init)
            xs = matmul_residual([o_s.reshape(sb, c_width).astype(BF16)], [wo], xs, sm[2],
                                 tm=sb, tn=1024, rows_per_group=sb)
            outs["sdk"].append(k3[:, None])
            outs["sdv"].append(v3[:, None])

        routed = peer_route(xp, n2, pm[3], pm[4], wq_all, keys_all, l, tm=ROUTE_TM, rows_per_group=seq)
        xp = peer_experts(*routed, u_all, vt_all, l, xp, pm[5], tm=PROMPT_TM, ni=EXPERT_NI, sub=EXPERT_SUB,
                          rows_per_group=seq)

        xs_pad = _pad_rows(xs, SAMPLE_PAD)
        smp = [jnp.pad(m_, ((0, 0), (0, SAMPLE_PAD - sb), (0, 0))) for m_ in sm[3:6]]
        routed = peer_route(xs_pad, n2, smp[0], smp[1], wq_all, keys_all, l, tm=SAMPLE_PAD, rows_per_group=SAMPLE_PAD)
        xs = peer_experts(*routed, u_all, vt_all, l, xs_pad, smp[2], tm=SAMPLE_PAD, ni=EXPERT_NI, sub=EXPERT_SUB,
                          rows_per_group=SAMPLE_PAD)[:sb]

    y_prompt = rmsnorm_rows(xp, final_norm_w[None], tm=PROMPT_TM).reshape(bsz, seq, d)
    y_sample = rmsnorm_rows(xs, final_norm_w[None], tm=sb).reshape(sb, st, d)
    stk = lambda k: jnp.stack(outs[k])
    kv_shape = (depth // 2, bsz, seq, h_c, 2 * HD_C)
    return (y_prompt, y_sample, stk("pwk"), stk("pwv"), stk("pcv"), stk("pss"),
            pdk_buf.reshape(kv_shape), pdv_buf.reshape(kv_shape),
            stk("swk"), stk("swv"), stk("scv"), stk("sss"), stk("sdk"), stk("sdv"))
```

```python
import functools
import math

import jax
import jax.numpy as jnp
from jax import lax
from jax.experimental import pallas as pl
from jax.experimental.pallas import tpu as pltpu

F32 = jnp.float32
BF16 = jnp.bfloat16
EPS = 1e-6
NEG_BIG = -1e30

HD_A = 128
HD_C = 128
DILATED_PATTERNS = ((128, 1), (512, 4), (2048, 16))
SSM_HEADDIM = 64
SSM_GROUPS = 4
D_STATE = 128
SSD_CHUNK = 128
PEER_TOPK = 16
ROUTE_HEADS_PER_TRIP = 4
DILATED_HEADS_PER_STEP = 2

V7X_VMEM_LIMIT = 58 * 1024 * 1024
LANE = 128
SUBLANE = 8
BF16_ROWS = 16

NT_DIMS = (((1,), (1,)), ((), ()))


def _cp(sem, vmem=V7X_VMEM_LIMIT):
    return pltpu.CompilerParams(dimension_semantics=sem, vmem_limit_bytes=vmem)


def _sigmoid(x):
    return 1.0 / (1.0 + jnp.exp(-x))


def _silu(x):
    return x * _sigmoid(x)


def _softplus(x):
    return jnp.maximum(x, 0.0) + jnp.log1p(jnp.exp(-jnp.abs(x)))


def _norm_modulate(x, nw, sh, sc):
    ms = jnp.mean(x * x, axis=-1, keepdims=True)
    y = x * lax.rsqrt(ms + EPS) * nw
    return y * (1.0 + sc) + sh


def _adaln_kernel(c_ref, w_ref, b_ref, o_ref):
    a_hi, a_lo = _split_bf16(_silu(c_ref[...]))
    w_hi, w_lo = _split_bf16(w_ref[0])
    o_ref[0] = _dot3(a_hi, a_lo, w_hi, w_lo, NN_DIMS) + b_ref[0]


def adaln_all(c_all, w_ada, b_ada):
    nl, d, n = w_ada.shape
    r = c_all.shape[0]
    tn = 1024
    return pl.pallas_call(
        _adaln_kernel,
        grid=(nl, n // tn),
        in_specs=[pl.BlockSpec((r, d), lambda l, j: (0, 0)),
                  pl.BlockSpec((1, d, tn), lambda l, j: (l, 0, j)),
                  pl.BlockSpec((1, 1, tn), lambda l, j: (l, 0, j))],
        out_specs=pl.BlockSpec((1, r, tn), lambda l, j: (l, 0, j)),
        out_shape=jax.ShapeDtypeStruct((nl, r, n), F32),
        compiler_params=_cp(("parallel", "parallel")),
        name="adaln",
    )(c_all, w_ada, b_ada.reshape(nl, 1, n))


def _normmod_mm_kernel(x_ref, nw_ref, sh_ref, sc_ref, w_ref, o_ref, ob_ref, h_ref):
    @pl.when(pl.program_id(1) == 0)
    def _():
        h_ref[...] = _norm_modulate(x_ref[...], nw_ref[...], sh_ref[0], sc_ref[0]).astype(BF16)

    o = jnp.dot(h_ref[...], w_ref[...], preferred_element_type=F32)
    o_ref[...] = o
    ob_ref[...] = o.astype(BF16)


def normmod_matmul(x, nw, sh, sc, w, *, tm, tn, rows_per_group):
    t, d = x.shape
    n = w.shape[1]
    r = sh.shape[1]
    tpg = rows_per_group // tm
    mod_spec = pl.BlockSpec((1, r, d), lambda i, j: (i // tpg, 0, 0))
    return pl.pallas_call(
        _normmod_mm_kernel,
        grid=(t // tm, n // tn),
        in_specs=[pl.BlockSpec((tm, d), lambda i, j: (i, 0)),
                  pl.BlockSpec((1, d), lambda i, j: (0, 0)),
                  mod_spec, mod_spec,
                  pl.BlockSpec((d, tn), lambda i, j: (0, j))],
        out_specs=[pl.BlockSpec((tm, tn), lambda i, j: (i, j)),
                   pl.BlockSpec((tm, tn), lambda i, j: (i, j))],
        out_shape=[jax.ShapeDtypeStruct((t, n), F32), jax.ShapeDtypeStruct((t, n), BF16)],
        scratch_shapes=[pltpu.VMEM((tm, d), BF16)],
        compiler_params=_cp(("parallel", "arbitrary")),
        name="normmod_matmul",
    )(x, nw, sh, sc, w)


def _normmod_mm_kv_kernel(x_ref, nw_ref, sh_ref, sc_ref, w_ref, kin_ref, vin_ref, ob_ref, k_ref, v_ref, h_ref, *, nseg):
    del kin_ref, vin_ref
    j = pl.program_id(1)

    @pl.when(j == 0)
    def _():
        h_ref[...] = _norm_modulate(x_ref[...], nw_ref[...], sh_ref[0], sc_ref[0]).astype(BF16)

    o = jnp.dot(h_ref[...], w_ref[...], preferred_element_type=F32)
    ob_ref[...] = o.astype(BF16)

    @pl.when((j >= nseg) & (j < 2 * nseg))
    def _():
        k_ref[0] = o

    @pl.when(j >= 2 * nseg)
    def _():
        v_ref[0] = o


def normmod_matmul_kv(x, nw, sh, sc, w, kbuf, vbuf, layer, *, tm, tn, rows_per_group):
    t, d = x.shape
    n = w.shape[1]
    r = sh.shape[1]
    tpg = rows_per_group // tm
    nseg = n // 3 // tn
    mod_spec = pl.BlockSpec((1, r, d), lambda i, j: (i // tpg, 0, 0))
    any_spec = pl.BlockSpec(memory_space=pl.ANY)
    k_spec = pl.BlockSpec((1, tm, tn), lambda i, j: (layer, i, jnp.clip(j - nseg, 0, nseg - 1)))
    v_spec = pl.BlockSpec((1, tm, tn), lambda i, j: (layer, i, jnp.clip(j - 2 * nseg, 0, nseg - 1)))
    return pl.pallas_call(
        functools.partial(_normmod_mm_kv_kernel, nseg=nseg),
        grid=(t // tm, n // tn),
        in_specs=[pl.BlockSpec((tm, d), lambda i, j: (i, 0), pipeline_mode=pl.Buffered(1)),
                  pl.BlockSpec((1, d), lambda i, j: (0, 0)),
                  mod_spec, mod_spec,
                  pl.BlockSpec((d, tn), lambda i, j: (0, j)),
                  any_spec, any_spec],
        out_specs=[pl.BlockSpec((tm, tn), lambda i, j: (i, j)), k_spec, v_spec],
        out_shape=[jax.ShapeDtypeStruct((t, n), BF16),
                   jax.ShapeDtypeStruct(kbuf.shape, F32), jax.ShapeDtypeStruct(vbuf.shape, F32)],
        input_output_aliases={5: 1, 6: 2},
        scratch_shapes=[pltpu.VMEM((tm, d), BF16)],
        compiler_params=_cp(("parallel", "arbitrary")),
        name="normmod_matmul_kv",
    )(x, nw, sh, sc, w, kbuf, vbuf)


def _mm_res_kernel(*refs, n_pairs):
    a_refs = refs[:n_pairs]
    w_refs = refs[n_pairs:2 * n_pairs]
    x_ref, g_ref, o_ref = refs[2 * n_pairs:]
    acc = jnp.dot(a_refs[0][...], w_refs[0][...], preferred_element_type=F32)
    for a_ref, w_ref in zip(a_refs[1:], w_refs[1:]):
        acc = acc + jnp.dot(a_ref[...], w_ref[...], preferred_element_type=F32)
    o_ref[...] = x_ref[...] + g_ref[0] * acc


def matmul_residual(a_list, w_list, x, gate, *, tm, tn, rows_per_group):
    t, n = x.shape
    r = gate.shape[1]
    tpg = rows_per_group // tm
    npairs = len(a_list)
    in_specs = ([pl.BlockSpec((tm, a.shape[1]), lambda i, j: (i, 0)) for a in a_list]
                + [pl.BlockSpec((w.shape[0], tn), lambda i, j: (0, j)) for w in w_list]
                + [pl.BlockSpec((tm, tn), lambda i, j: (i, j)),
                   pl.BlockSpec((1, r, tn), lambda i, j: (i // tpg, 0, j))])
    return pl.pallas_call(
        functools.partial(_mm_res_kernel, n_pairs=npairs),
        grid=(t // tm, n // tn),
        in_specs=in_specs,
        out_specs=pl.BlockSpec((tm, tn), lambda i, j: (i, j)),
        out_shape=jax.ShapeDtypeStruct((t, n), F32),
        compiler_params=_cp(("parallel", "parallel")),
        name="matmul_residual",
    )(*a_list, *w_list, x, gate)


def _rms_kernel(x_ref, w_ref, o_ref):
    x = x_ref[...]
    ms = jnp.mean(x * x, axis=-1, keepdims=True)
    o_ref[...] = x * lax.rsqrt(ms + EPS) * w_ref[...]


def rmsnorm_rows(x, w, *, tm):
    t, d = x.shape
    return pl.pallas_call(
        _rms_kernel,
        grid=(t // tm,),
        in_specs=[pl.BlockSpec((tm, d), lambda i: (i, 0)), pl.BlockSpec((1, d), lambda i: (0, 0))],
        out_specs=pl.BlockSpec((tm, d), lambda i: (i, 0)),
        out_shape=jax.ShapeDtypeStruct((t, d), F32),
        compiler_params=_cp(("parallel",)),
        name="final_rmsnorm",
    )(x, w)


def _top_values_ranked(s, k, want_rank):
    vals = []
    rank = jnp.full(s.shape, float(k), F32) if want_rank else None
    for it in range(k):
        m = jnp.max(s, axis=0, keepdims=True)
        hit = s == m
        if want_rank:
            rank = jnp.where(hit, float(it), rank)
        vals.append(m)
        if it + 1 < k:
            s = jnp.where(hit, -jnp.inf, s)
    return vals, rank


def _split_bf16(a):
    hi = a.astype(BF16)
    lo = (a - hi.astype(F32)).astype(BF16)
    return hi, lo


def _dot3(a_hi, a_lo, b_hi, b_lo, dims):
    f = functools.partial(lax.dot_general, dimension_numbers=dims, preferred_element_type=F32)
    return f(a_hi, b_hi) + (f(a_hi, b_lo) + f(a_lo, b_hi))


NN_DIMS = (((1,), (0,)), ((), ()))


def _peer_route_kernel(x_ref, nw_ref, sh_ref, sc_ref, wqh_ref, wql_ref, kh_ref, kl_ref,
                       h_ref, cnt_ref, r_ref, rank_ref, e2_ref, qt_ref, *, heads, topk):
    h = _norm_modulate(x_ref[...], nw_ref[...], sh_ref[0], sc_ref[0])
    hb, hl = _split_bf16(h)
    h_ref[...] = hb
    qt_ref[...] = _dot3(wqh_ref[0], wql_ref[0], hb, hl, NT_DIMS)
    nk = kh_ref.shape[2]
    tm = hb.shape[0]
    half = topk // 2
    row8 = lax.broadcasted_iota(jnp.int32, (half, tm), 0)

    def head_body(hd, carry):
        s = []
        for c in range(2):
            idx = hd * 2 + c
            qh, ql = _split_bf16(qt_ref[pl.ds(pl.multiple_of(idx * nk, nk), nk), :])
            s.append(_dot3(kh_ref[0, idx], kl_ref[0, idx], qh, ql, NN_DIMS))
        v1, _ = _top_values_ranked(s[0], topk, False)
        v2, rank2 = _top_values_ranked(s[1], topk, True)
        v1s = jnp.concatenate(v1, axis=0)
        v2s = jnp.concatenate(v2, axis=0)
        pieces = [v1[0] + v2s, v1[1] + v2s[0:half]]
        for a in range(2, half):
            pieces.append(jnp.where(row8 < topk // (a + 1), v1[a] + v2s[0:half], -jnp.inf))
        pieces.append(v1s[half:topk] + v2[0])
        cand = jnp.concatenate(pieces, axis=0)
        mx = v1[0] + v2[0]
        z = jnp.zeros_like(mx)
        tau = mx
        for it in range(topk):
            tau = jnp.max(cand, axis=0, keepdims=True)
            z = z + jnp.exp(tau - mx)
            if it + 1 < topk:
                cand = jnp.where(cand == tau, -jnp.inf, cand)
        cnt = jnp.zeros(s[0].shape, F32)
        for b in range(half):
            cnt = cnt + jnp.where(s[0] + v2[b] >= tau, 1.0, 0.0)
        extra = jnp.zeros_like(tau)
        for b in range(half, topk):
            extra = extra + jnp.where(v1[0] + v2[b] >= tau, 1.0, 0.0)
        cnt_ref[hd] = cnt + jnp.where(s[0] == v1[0], extra, 0.0)
        r_ref[hd] = jnp.exp(s[0] - v1[0]) / z
        rank_ref[hd] = rank2.astype(BF16)
        e2_ref[hd] = jnp.exp(s[1] - v2[0]).astype(BF16)
        return carry

    def group_body(hg, carry):
        for k in range(ROUTE_HEADS_PER_TRIP):
            head_body(ROUTE_HEADS_PER_TRIP * hg + k, carry)
        return carry

    lax.fori_loop(0, heads // ROUTE_HEADS_PER_TRIP, group_body, 0)


def peer_route(x, nw, sh, sc, wqt, keys2, layer, *, tm, rows_per_group):
    t, d = x.shape
    r = sh.shape[1]
    nq = wqt[0].shape[1]
    _, nhc, nk, dh = keys2[0].shape
    heads = nhc // 2
    tpg = rows_per_group // tm
    mod_spec = pl.BlockSpec((1, r, d), lambda i: (i // tpg, 0, 0))
    fac_spec = pl.BlockSpec((heads, nk, tm), lambda i: (0, 0, i))
    f32_shape = jax.ShapeDtypeStruct((heads, nk, t), F32)
    bf_shape = jax.ShapeDtypeStruct((heads, nk, t), BF16)
    wq_spec = pl.BlockSpec((1, nq, d), lambda i: (layer, 0, 0), pipeline_mode=pl.Buffered(1))
    key_spec = pl.BlockSpec((1, nhc, nk, dh), lambda i: (layer, 0, 0, 0), pipeline_mode=pl.Buffered(1))
    return pl.pallas_call(
        functools.partial(_peer_route_kernel, heads=heads, topk=PEER_TOPK),
        grid=(t // tm,),
        in_specs=[pl.BlockSpec((tm, d), lambda i: (i, 0)),
                  pl.BlockSpec((1, d), lambda i: (0, 0)),
                  mod_spec, mod_spec, wq_spec, wq_spec, key_spec, key_spec],
        out_specs=[pl.BlockSpec((tm, d), lambda i: (i, 0)), fac_spec, fac_spec, fac_spec, fac_spec],
        out_shape=[jax.ShapeDtypeStruct((t, d), BF16), f32_shape, f32_shape, bf_shape, bf_shape],
        scratch_shapes=[pltpu.VMEM((nq, tm), F32)],
        compiler_params=_cp(("parallel",)),
        name="peer_route",
    )(x, nw, sh, sc, wqt[0], wqt[1], keys2[0], keys2[1])


def _gelu_exact(a):
    return 0.5 * a * (1.0 + lax.erf(a * (1.0 / math.sqrt(2.0))))


def _peer_expert_kernel(h_ref, cnt_ref, r_ref, rank_ref, e2_ref, *rest, heads, ni, sub, nv):
    ns = ni // sub
    u_refs, vt_refs = rest[:ns], rest[ns:ns + nv]
    x_ref, g_ref, o_ref, acc_ref, wa_ref = rest[ns + nv:]
    j = pl.program_id(1)
    nk = rank_ref.shape[1] * rank_ref.shape[2]

    @pl.when(j == 0)
    def _():
        acc_ref[...] = jnp.zeros_like(acc_ref)

    hb = h_ref[...]
    tm = hb.shape[0]
    pk = BF16_ROWS
    for c in range(ns):
        lo, hi = c * sub * nk, (c + 1) * sub * nk
        act = lax.dot_general(u_refs[c][0], hb, NT_DIMS, preferred_element_type=F32)
        act = _gelu_exact(act).astype(BF16)
        gs = []
        for ii in range(c * sub, (c + 1) * sub):
            g = None
            for hd in range(heads):
                cnt = jnp.broadcast_to(cnt_ref[hd, ii:ii + 1, :], (pk, tm)).astype(BF16)[None]
                rr = jnp.broadcast_to(r_ref[hd, ii:ii + 1, :], (pk, tm)).astype(BF16)[None]
                contrib = jnp.where(rank_ref[hd] < cnt, e2_ref[hd] * rr, jnp.zeros((), BF16))
                g = contrib if g is None else g + contrib
            gs.append(g.reshape(nk, tm))
        wa_ref[lo:hi, :] = jnp.concatenate(gs, axis=0) * act
    wa = wa_ref[...]
    dr = acc_ref.shape[0] // nv
    for c in range(nv):
        acc_ref[c * dr:(c + 1) * dr, :] += jnp.dot(vt_refs[c][0], wa, preferred_element_type=F32)

    @pl.when(j == pl.num_programs(1) - 1)
    def _():
        o_ref[...] = x_ref[...] + g_ref[0] * acc_ref[...].T


def peer_experts(h, cnt, rfac, rank, e2, u, vt, layer, x, gate, *, tm, ni, sub, rows_per_group):
    t, d = x.shape
    heads, nk, _ = cnt.shape
    r = gate.shape[1]
    tpg = rows_per_group // tm
    te = ni * nk
    row_spec = pl.BlockSpec((heads, ni, tm), lambda i, j: (0, j, i))
    full_spec = pl.BlockSpec((heads, nk // BF16_ROWS, BF16_ROWS, tm), lambda i, j: (0, 0, 0, i))
    rank = rank.reshape(heads, nk // BF16_ROWS, BF16_ROWS, t)
    e2 = e2.reshape(heads, nk // BF16_ROWS, BF16_ROWS, t)
    ns = ni // sub
    nv = ns if tm <= LANE else 1
    u_specs = [pl.BlockSpec((1, te // ns, d), lambda i, j, c=c: (layer, j * ns + c, 0)) for c in range(ns)]
    vt_specs = [pl.BlockSpec((1, d // nv, te), lambda i, j, c=c: (layer, c, j)) for c in range(nv)]
    return pl.pallas_call(
        functools.partial(_peer_expert_kernel, heads=heads, ni=ni, sub=sub, nv=nv),
        grid=(t // tm, nk // ni),
        in_specs=[pl.BlockSpec((tm, d), lambda i, j: (i, 0)),
                  row_spec, row_spec, full_spec, full_spec, *u_specs, *vt_specs,
                  pl.BlockSpec((tm, d), lambda i, j: (i, 0), pipeline_mode=pl.Buffered(1)),
                  pl.BlockSpec((1, r, d), lambda i, j: (i // tpg, 0, 0))],
        out_specs=pl.BlockSpec((tm, d), lambda i, j: (i, 0)),
        out_shape=jax.ShapeDtypeStruct((t, d), F32),
        scratch_shapes=[pltpu.VMEM((d, tm), F32), pltpu.VMEM((te, tm), BF16)],
        compiler_params=_cp(("parallel", "arbitrary")),
        name="peer_experts",
    )(h, cnt, rfac, rank, e2, *([u] * ns), *([vt] * nv), x, gate)


def _dilated_mult_table(tq, tk, n_off):
    o = jnp.arange(n_off, dtype=jnp.int32)[:, None, None]
    r = jnp.arange(tq, dtype=jnp.int32)[None, :, None]
    c = jnp.arange(tk, dtype=jnp.int32)[None, None, :]
    d = o * tk + r - c
    mult = jnp.zeros((n_off, tq, tk), F32)
    for win, dil in DILATED_PATTERNS:
        steps = win // dil
        ok = (d >= 0) & (d % dil == 0) & (d <= steps * dil)
        mult = mult + ok.astype(F32)
    return mult


def _dil_attn_kernel(q_ref, k_ref, v_ref, mult_ref, o_ref, m_ref, l_ref, acc_ref, *, t, n_off, scale, hpb):
    qi = pl.program_id(2)
    q = (q_ref[...].astype(F32) * scale).astype(BF16)
    m_ref[...] = jnp.full_like(m_ref, NEG_BIG)
    l_ref[...] = jnp.zeros_like(l_ref)
    acc_ref[...] = jnp.zeros_like(acc_ref)

    def body(kb, carry):
        off = qi - kb
        start = pl.multiple_of(kb * t, t)
        k = k_ref[pl.ds(start, t), :]
        v = v_ref[pl.ds(start, t), :]
        mult = mult_ref[off]
        ss = [lax.dot_general(q[:, hh * HD_A:(hh + 1) * HD_A], k[:, hh * HD_A:(hh + 1) * HD_A], NT_DIMS,
                              preferred_element_type=F32) for hh in range(hpb)]
        for hh in range(hpb):
            s = jnp.where(mult > 0.0, ss[hh], NEG_BIG)
            m_old = m_ref[hh]
            m_new = jnp.maximum(m_old, jnp.max(s, axis=-1, keepdims=True))
            alpha = jnp.exp(m_old - m_new)
            p = jnp.exp(s - jnp.tile(m_new, (1, t // LANE))) * mult
            l_ref[hh] = alpha * l_ref[hh] + jnp.sum(p, axis=-1, keepdims=True)
            acc_ref[hh] = alpha * acc_ref[hh] + jnp.dot(p.astype(BF16), v[:, hh * HD_A:(hh + 1) * HD_A],
                                                        preferred_element_type=F32)
            m_ref[hh] = m_new
        return carry

    lax.fori_loop(jnp.maximum(qi - (n_off - 1), 0), qi + 1, body, 0)
    for hh in range(hpb):
        o_ref[:, hh * HD_A:(hh + 1) * HD_A] = (acc_ref[hh] / l_ref[hh]).astype(o_ref.dtype)


def dilated_attention_prompt(proj_bf, *, batch, seq, heads, t):
    win_max = min(max(w for w, _ in DILATED_PATTERNS), seq)
    n_off = min(-(-win_max // t) + 1, seq // t)
    mult = _dilated_mult_table(t, t, n_off)
    nq = seq // t
    hpb = DILATED_HEADS_PER_STEP
    hw = hpb * HD_A
    hg = heads // hpb
    kern = functools.partial(_dil_attn_kernel, t=t, n_off=n_off, scale=1.0 / math.sqrt(HD_A), hpb=hpb)
    return pl.pallas_call(
        kern,
        grid=(batch, hg, nq),
        in_specs=[pl.BlockSpec((t, hw), lambda b, h, i: (b * nq + i, h)),
                  pl.BlockSpec((seq, hw), lambda b, h, i: (b, hg + h)),
                  pl.BlockSpec((seq, hw), lambda b, h, i: (b, 2 * hg + h)),
                  pl.BlockSpec((n_off, t, t), lambda b, h, i: (0, 0, 0))],
        out_specs=pl.BlockSpec((t, hw), lambda b, h, i: (b * nq + i, h)),
        out_shape=jax.ShapeDtypeStruct((batch * seq, heads * HD_A), BF16),
        scratch_shapes=[pltpu.VMEM((hpb, t, LANE), F32), pltpu.VMEM((hpb, t, LANE), F32),
                        pltpu.VMEM((hpb, t, HD_A), F32)],
        compiler_params=_cp(("parallel", "parallel", "arbitrary")),
        name="dilated_attention",
    )(proj_bf, proj_bf, proj_bf, mult)


def _pair_cols(q, p, lane_lo):
    a = q[:, 2 * p:2 * p + 1]
    b = q[:, 2 * p + 1:2 * p + 2]
    return jnp.where(lane_lo, a, b)


def _ssd_kernel(z_ref, xbc_ref, dt_ref, cw_ref, cb_ref, dtb_ref, alog_ref, dsk_ref, nw_ref,
                y_ref, st_ref, xprev_ref, s_ref, *, d_inner, n_pairs, conv_w):
    c = pl.program_id(1)
    q = xbc_ref.shape[0]
    pre = SUBLANE

    @pl.when(c == 0)
    def _():
        xprev_ref[0:pre, :] = jnp.zeros((pre, xprev_ref.shape[1]), F32)
        s_ref[...] = jnp.zeros_like(s_ref)

    x_in = xbc_ref[...]
    xprev_ref[pre:pre + q, :] = x_in
    conv = cb_ref[...]
    for kk in range(conv_w):
        conv = conv + xprev_ref[pl.ds(pre - (conv_w - 1) + kk, q), :] * cw_ref[kk:kk + 1, :]
    xprev_ref[0:pre, :] = x_in[q - pre:q, :]
    act = _silu(conv)
    gn = SSM_GROUPS * D_STATE
    xs = act[:, :d_inner]
    bm = act[:, d_inner:d_inner + gn]
    cm = act[:, d_inner + gn:d_inner + 2 * gn]

    dt = _softplus(dt_ref[...] + dtb_ref[...])
    a = -jnp.exp(alog_ref[...])
    dta = dt * a
    row = lax.broadcasted_iota(jnp.int32, (q, q), 0)
    col = lax.broadcasted_iota(jnp.int32, (q, q), 1)
    causal = row >= col
    cum = jnp.dot(causal.astype(F32), dta, preferred_element_type=F32, precision=lax.Precision.HIGHEST)
    cum_t = cum.T
    dt_t = dt.T
    cum_end = cum[q - 1:q, :]
    wdec = jnp.exp(cum_end - cum) * dt
    ecum = jnp.exp(cum)
    cdec = jnp.exp(cum_end)
    lane_lo = lax.broadcasted_iota(jnp.int32, (q, LANE), 1) < SSM_HEADDIM
    lane_lo1 = lane_lo[0:1, :]
    pairs_per_group = (n_pairs * 2 // SSM_GROUPS) // 2

    ys = []
    for g in range(SSM_GROUPS):
        bg = bm[:, g * D_STATE:(g + 1) * D_STATE]
        cg = cm[:, g * D_STATE:(g + 1) * D_STATE].astype(BF16)
        cb = lax.dot_general(cg, bg.astype(BF16), NT_DIMS, preferred_element_type=F32)
        bg_t = bg.T.astype(BF16)
        for pp in range(pairs_per_group):
            p = g * pairs_per_group + pp
            xp = xs[:, p * LANE:(p + 1) * LANE]
            xpb = xp.astype(BF16)
            y_intra = None
            for hh in range(2):
                h = 2 * p + hh
                seg = cum[:, h:h + 1] - cum_t[h:h + 1, :]
                dec = jnp.exp(jnp.where(causal, seg, -jnp.inf))
                mm = (cb * dec * dt_t[h:h + 1, :]).astype(BF16)
                yh = jnp.dot(mm, xpb, preferred_element_type=F32)
                y_intra = yh if y_intra is None else jnp.where(lane_lo, y_intra, yh)
            sp = s_ref[p]
            y_inter = jnp.dot(cg, sp.astype(BF16), preferred_element_type=F32) * _pair_cols(ecum, p, lane_lo)
            ys.append(y_intra + y_inter + dsk_ref[:, p * LANE:(p + 1) * LANE] * xp)
            xw = (xp * _pair_cols(wdec, p, lane_lo)).astype(BF16)
            s_ref[p] = (_pair_cols(cdec, p, lane_lo1) * sp
                        + jnp.dot(bg_t, xw, preferred_element_type=F32))
    y = jnp.concatenate(ys, axis=1)
    y = y * _silu(z_ref[...])
    gw = d_inner // SSM_GROUPS
    outs = []
    for g in range(SSM_GROUPS):
        yg = y[:, g * gw:(g + 1) * gw]
        outs.append(yg * lax.rsqrt(jnp.mean(yg * yg, axis=-1, keepdims=True) + EPS))
    y_ref[...] = (jnp.concatenate(outs, axis=1) * nw_ref[...]).astype(y_ref.dtype)

    @pl.when(c == pl.num_programs(1) - 1)
    def _():
        st_ref[0] = s_ref[...]


def ssd_prompt(proj, conv_w, conv_b, dtb_pad, alog_pad, dskip_lane, norm_w, *, batch, seq, d_inner, col0):
    q = SSD_CHUNK
    nc = seq // q
    conv_dim = conv_w.shape[1]
    cw = conv_w.shape[0]
    n_pairs = d_inner // (2 * SSM_HEADDIM)
    zb = col0 // d_inner
    xb = (col0 + d_inner) // conv_dim
    db = (col0 + d_inner + conv_dim) // LANE
    kern = functools.partial(_ssd_kernel, d_inner=d_inner, n_pairs=n_pairs, conv_w=cw)
    const = lambda b, c: (0, 0)
    return pl.pallas_call(
        kern,
        grid=(batch, nc),
        in_specs=[pl.BlockSpec((q, d_inner), lambda b, c: (b * nc + c, zb)),
                  pl.BlockSpec((q, conv_dim), lambda b, c: (b * nc + c, xb)),
                  pl.BlockSpec((q, LANE), lambda b, c: (b * nc + c, db)),
                  pl.BlockSpec((cw, conv_dim), const),
                  pl.BlockSpec((1, conv_dim), const),
                  pl.BlockSpec((1, LANE), const),
                  pl.BlockSpec((1, LANE), const),
                  pl.BlockSpec((1, d_inner), const),
                  pl.BlockSpec((1, d_inner), const)],
        out_specs=[pl.BlockSpec((q, d_inner), lambda b, c: (b * nc + c, 0)),
                   pl.BlockSpec((1, n_pairs, D_STATE, LANE), lambda b, c: (b, 0, 0, 0))],
        out_shape=[jax.ShapeDtypeStruct((batch * seq, d_inner), BF16),
                   jax.ShapeDtypeStruct((batch, n_pairs, D_STATE, LANE), F32)],
        scratch_shapes=[pltpu.VMEM((SUBLANE + q, conv_dim), F32),
                        pltpu.VMEM((n_pairs, D_STATE, LANE), F32)],
        compiler_params=_cp(("parallel", "arbitrary")),
        name="ssd_prompt",
    )(proj, proj, proj, conv_w, conv_b, dtb_pad, alog_pad, dskip_lane, norm_w)


def _state_from_pairs(st):
    b, npair, n, _ = st.shape
    return st.reshape(b, npair, n, 2, SSM_HEADDIM).transpose(0, 1, 3, 4, 2).reshape(b, 2 * npair, SSM_HEADDIM, n)


def _diff_lambda(lamp, lam_init):
    l1 = jnp.sum(lamp[0:1] * lamp[1:2], axis=-1, keepdims=True)
    l2 = jnp.sum(lamp[2:3] * lamp[3:4], axis=-1, keepdims=True)
    return jnp.exp(l1) - jnp.exp(l2) + lam_init


def _diff_finish(acc0, l0, acc1, l1, lam, sw, lam_init):
    o = acc0 / l0 - lam * (acc1 / l1)
    o = o * lax.rsqrt(jnp.mean(o * o, axis=-1, keepdims=True) + EPS) * sw
    return o * (1.0 - lam_init)


def _diff_attn_kernel(lamp_ref, sw_ref, q_ref, k_ref, v_ref, o_ref, m_ref, l_ref, acc_ref, *, t, scale, lam_init):
    qi = pl.program_id(2)
    q = (q_ref[...].astype(F32) * scale).astype(BF16)
    m_ref[...] = jnp.full_like(m_ref, NEG_BIG)
    l_ref[...] = jnp.zeros_like(l_ref)
    acc_ref[...] = jnp.zeros_like(acc_ref)
    row = lax.broadcasted_iota(jnp.int32, (t, t), 0)
    col = lax.broadcasted_iota(jnp.int32, (t, t), 1)

    def step(kb, masked):
        start = pl.multiple_of(kb * t, t)
        k = k_ref[pl.ds(start, t), :]
        v = v_ref[pl.ds(start, t), :]
        def scores(m):
            return lax.dot_general(q[:, m * HD_C:(m + 1) * HD_C], k[:, m * HD_C:(m + 1) * HD_C], NT_DIMS,
                                   preferred_element_type=F32)

        ss = None if masked else [scores(0), scores(1)]
        for m in range(2):
            s = scores(m) if masked else ss[m]
            if masked:
                s = jnp.where(row >= col, s, NEG_BIG)
            m_old = m_ref[m]
            m_new = jnp.maximum(m_old, jnp.max(s, axis=-1, keepdims=True))
            alpha = jnp.exp(m_old - m_new)
            p = jnp.exp(s - jnp.tile(m_new, (1, t // LANE)))
            l_ref[m] = alpha * l_ref[m] + jnp.sum(p, axis=-1, keepdims=True)
            acc_ref[m] = (jnp.tile(alpha, (1, v.shape[1] // LANE)) * acc_ref[m]
                          + jnp.dot(p.astype(BF16), v, preferred_element_type=F32))
            m_ref[m] = m_new

    def body(kb, carry):
        step(kb, False)
        return carry

    lax.fori_loop(0, qi, body, 0)
    step(qi, True)
    lam = _diff_lambda(lamp_ref[...], lam_init)
    o_ref[...] = _diff_finish(acc_ref[0], l_ref[0][:, 0:1], acc_ref[1], l_ref[1][:, 0:1], lam, sw_ref[...],
                              lam_init).astype(o_ref.dtype)


def diff_attention_prompt(proj_bf, lamp, subln_w, *, batch, seq, heads, t, lam_init):
    hw = 2 * HD_C
    nq = seq // t
    kern = functools.partial(_diff_attn_kernel, t=t, scale=1.0 / math.sqrt(HD_C), lam_init=lam_init)
    return pl.pallas_call(
        kern,
        grid=(batch, heads, nq),
        in_specs=[pl.BlockSpec((4, HD_C), lambda b, h, i: (0, 0)),
                  pl.BlockSpec((1, hw), lambda b, h, i: (0, 0)),
                  pl.BlockSpec((t, hw), lambda b, h, i: (b * nq + i, h)),
                  pl.BlockSpec((seq, hw), lambda b, h, i: (b, heads + h)),
                  pl.BlockSpec((seq, hw), lambda b, h, i: (b, 2 * heads + h))],
        out_specs=pl.BlockSpec((t, hw), lambda b, h, i: (b * nq + i, h)),
        out_shape=jax.ShapeDtypeStruct((batch * seq, heads * hw), BF16),
        scratch_shapes=[pltpu.VMEM((2, t, LANE), F32), pltpu.VMEM((2, t, LANE), F32), pltpu.VMEM((2, t, hw), F32)],
        compiler_params=_cp(("parallel", "parallel", "arbitrary")),
        name="diff_attention",
    )(lamp, subln_w, proj_bf, proj_bf, proj_bf)


def _diff_sample_kernel(pt_ref, q_ref, kn_ref, vn_ref, lamp_ref, sw_ref, *rest, pp, scale, lam_init):
    del pt_ref
    k_refs = rest[:pp]
    v_refs = rest[pp:2 * pp]
    o_ref, m_ref, l_ref, acc_ref = rest[2 * pp:]
    pg = pl.program_id(1)
    q = q_ref[0] * scale

    @pl.when(pg == 0)
    def _():
        prod = q * kn_ref[0]
        for m in range(2):
            m_ref[m] = jnp.sum(prod[:, m * HD_C:(m + 1) * HD_C], axis=-1, keepdims=True)
            l_ref[m] = jnp.ones_like(l_ref[m])
            acc_ref[m] = vn_ref[0]

    for i in range(pp):
        kk = k_refs[i][0, 0]
        vv = v_refs[i][0, 0]
        prod = kk * q[None]
        for m in range(2):
            s = jnp.sum(prod[:, :, m * HD_C:(m + 1) * HD_C], axis=-1, keepdims=True)
            m_old = m_ref[m]
            m_new = jnp.maximum(m_old, jnp.max(s, axis=0))
            alpha = jnp.exp(m_old - m_new)
            p = jnp.exp(s - m_new[None])
            l_ref[m] = alpha * l_ref[m] + jnp.sum(p, axis=0)
            acc_ref[m] = alpha * acc_ref[m] + jnp.sum(p * vv, axis=0)
            m_ref[m] = m_new

    @pl.when(pg == pl.num_programs(1) - 1)
    def _():
        lam = _diff_lambda(lamp_ref[...], lam_init)
        o_ref[0] = _diff_finish(acc_ref[0], l_ref[0], acc_ref[1], l_ref[1], lam, sw_ref[...], lam_init)


def diff_attention_sample(page_table, q3, kn3, vn3, lamp, subln_w, cache_k, cache_v, *, layer, pp, lam_init):
    bsz, heads, hw = q3.shape
    n_pages = page_table.shape[1]
    page = cache_k.shape[2]
    row_spec = pl.BlockSpec((1, heads, hw), lambda b, p, pt: (b, 0, 0))

    def page_spec(i):
        return pl.BlockSpec((1, 1, page, heads, hw), lambda b, p, pt: (layer, pt[b, p * pp + i], 0, 0, 0))

    kern = functools.partial(_diff_sample_kernel, pp=pp, scale=1.0 / math.sqrt(HD_C), lam_init=lam_init)
    grid_spec = pltpu.PrefetchScalarGridSpec(
        num_scalar_prefetch=1,
        grid=(bsz, n_pages // pp),
        in_specs=[row_spec, row_spec, row_spec,
                  pl.BlockSpec((4, HD_C), lambda b, p, pt: (0, 0)),
                  pl.BlockSpec((1, hw), lambda b, p, pt: (0, 0))]
                 + [page_spec(i) for i in range(pp)] + [page_spec(i) for i in range(pp)],
        out_specs=row_spec,
        scratch_shapes=[pltpu.VMEM((2, heads, 1), F32), pltpu.VMEM((2, heads, 1), F32),
                        pltpu.VMEM((2, heads, hw), F32)],
    )
    return pl.pallas_call(
        kern,
        grid_spec=grid_spec,
        out_shape=jax.ShapeDtypeStruct((bsz, heads, hw), F32),
        compiler_params=_cp(("parallel", "arbitrary")),
        name="diff_attention_sample",
    )(page_table, q3, kn3, vn3, lamp, subln_w, *([cache_k] * pp), *([cache_v] * pp))


def _win_sample_kernel(q_ref, kn_ref, vn_ref, ck_ref, cv_ref, o_ref, m_ref, l_ref, acc_ref, *, tr, wb, scale):
    r = pl.program_id(1)
    q = q_ref[0] * scale

    @pl.when(r == 0)
    def _():
        m_ref[...] = jnp.sum(q * kn_ref[0], axis=-1, keepdims=True)
        l_ref[...] = jnp.full_like(l_ref, float(len(DILATED_PATTERNS)))
        acc_ref[...] = float(len(DILATED_PATTERNS)) * vn_ref[0]

    kk = ck_ref[0, 0]
    s = jnp.sum(kk * q[None], axis=-1, keepdims=True)
    j = r * tr + lax.broadcasted_iota(jnp.int32, s.shape, 0)
    d = wb - j
    mult = jnp.zeros(s.shape, F32)
    for win, dil in DILATED_PATTERNS:
        steps = win // dil
        mult = mult + ((d % dil == 0) & (d <= steps * dil)).astype(F32)
    s = jnp.where(mult > 0.0, s, NEG_BIG)
    m_old = m_ref[...]
    m_new = jnp.maximum(m_old, jnp.max(s, axis=0))
    alpha = jnp.exp(m_old - m_new)
    p = jnp.exp(s - m_new[None]) * mult
    l_ref[...] = alpha * l_ref[...] + jnp.sum(p, axis=0)
    acc_ref[...] = alpha * acc_ref[...] + jnp.sum(p * cv_ref[0, 0], axis=0)
    m_ref[...] = m_new

    @pl.when(r == pl.num_programs(1) - 1)
    def _():
        o_ref[0] = acc_ref[...] / l_ref[...]


def dilated_attention_sample(q3, kn3, vn3, cache_k, cache_v, layer, *, tr):
    bsz, heads, hd = q3.shape
    wb = cache_k.shape[2]
    row_spec = pl.BlockSpec((1, heads, hd), lambda b, r: (b, 0, 0))
    c_spec = pl.BlockSpec((1, 1, tr, heads, hd), lambda b, r: (layer, b, r, 0, 0))
    kern = functools.partial(_win_sample_kernel, tr=tr, wb=wb, scale=1.0 / math.sqrt(hd))
    return pl.pallas_call(
        kern,
        grid=(bsz, wb // tr),
        in_specs=[row_spec, row_spec, row_spec, c_spec, c_spec],
        out_specs=row_spec,
        out_shape=jax.ShapeDtypeStruct((bsz, heads, hd), F32),
        scratch_shapes=[pltpu.VMEM((heads, 1), F32), pltpu.VMEM((heads, 1), F32), pltpu.VMEM((heads, hd), F32)],
        compiler_params=_cp(("parallel", "arbitrary")),
        name="dilated_attention_sample",
    )(q3, kn3, vn3, cache_k, cache_v)


def _lane_to_col(v):
    n = v.shape[1] // LANE
    return jnp.concatenate([jnp.broadcast_to(v[:, k * LANE:(k + 1) * LANE], (LANE, LANE)).T for k in range(n)], axis=0)


def _col_to_lane(c):
    n = c.shape[0] // LANE
    return jnp.concatenate([jnp.broadcast_to(c[k * LANE:(k + 1) * LANE], (LANE, LANE)).T[0:1] for k in range(n)], axis=1)


def _ssd_step_kernel(z_ref, xbc_ref, dt_ref, cs_ref, st_ref, cw_ref, cb_ref, dtb_ref, alog_ref, dsk_ref, nw_ref,
                     e_ref, y_ref, cso_ref, sto_ref, *, d_inner):
    x = xbc_ref[0]
    cs = cs_ref[0]
    ncs = cs.shape[0]
    conv = cb_ref[...] + x * cw_ref[ncs:ncs + 1, :]
    for kk in range(ncs):
        conv = conv + cs[kk:kk + 1, :] * cw_ref[kk:kk + 1, :]
    cso_ref[0, 0:ncs - 1, :] = cs[1:ncs, :]
    cso_ref[0, ncs - 1:ncs, :] = x
    act = _silu(conv)
    gn = SSM_GROUPS * D_STATE
    xs = act[:, :d_inner]
    bm = act[:, d_inner:d_inner + gn]
    cm = act[:, d_inner + gn:d_inner + 2 * gn]
    dt = _softplus(dt_ref[0] + dtb_ref[...])
    dt_lane = jnp.dot(jnp.broadcast_to(dt, (SUBLANE, LANE)), e_ref[...], preferred_element_type=F32,
                      precision=lax.Precision.HIGHEST)[0:1]
    dec_lane = jnp.exp(dt_lane * (-jnp.exp(alog_ref[...])))
    dec_col = _lane_to_col(dec_lane)
    dtx_col = _lane_to_col(dt_lane * xs)
    rows_g = d_inner // SSM_GROUPS
    b_rows = jnp.concatenate([jnp.broadcast_to(bm[:, g * D_STATE:(g + 1) * D_STATE], (rows_g, D_STATE))
                              for g in range(SSM_GROUPS)], axis=0)
    c_rows = jnp.concatenate([jnp.broadcast_to(cm[:, g * D_STATE:(g + 1) * D_STATE], (rows_g, D_STATE))
                              for g in range(SSM_GROUPS)], axis=0)
    st = st_ref[0].reshape(d_inner, D_STATE)
    st_new = dec_col * st + dtx_col * b_rows
    sto_ref[0] = st_new.reshape(sto_ref.shape[1:])
    y = _col_to_lane(jnp.sum(st_new * c_rows, axis=-1, keepdims=True))
    y = y + dsk_ref[...] * xs
    y = y * _silu(z_ref[0])
    gw = d_inner // SSM_GROUPS
    outs = []
    for g in range(SSM_GROUPS):
        yg = y[:, g * gw:(g + 1) * gw]
        outs.append(yg * lax.rsqrt(jnp.mean(yg * yg, axis=-1, keepdims=True) + EPS))
    y_ref[0] = jnp.concatenate(outs, axis=1) * nw_ref[...]


def ssd_step(z3, xbc3, dt3, conv_state, ssm_state, conv_w, conv_b, dtb_pad, alog_lane, dskip_lane, norm_w, expand):
    bsz, _, d_inner = z3.shape
    conv_dim = xbc3.shape[2]
    ncs = conv_state.shape[1]
    hb, hp, n = ssm_state.shape[1:]
    const = lambda b: (0, 0)
    kern = functools.partial(_ssd_step_kernel, d_inner=d_inner)
    return pl.pallas_call(
        kern,
        grid=(bsz,),
        in_specs=[pl.BlockSpec((1, 1, d_inner), lambda b: (b, 0, 0)),
                  pl.BlockSpec((1, 1, conv_dim), lambda b: (b, 0, 0)),
                  pl.BlockSpec((1, 1, LANE), lambda b: (b, 0, 0)),
                  pl.BlockSpec((1, ncs, conv_dim), lambda b: (b, 0, 0)),
                  pl.BlockSpec((1, hb, hp, n), lambda b: (b, 0, 0, 0)),
                  pl.BlockSpec((ncs + 1, conv_dim), const),
                  pl.BlockSpec((1, conv_dim), const),
                  pl.BlockSpec((1, LANE), const),
                  pl.BlockSpec((1, d_inner), const),
                  pl.BlockSpec((1, d_inner), const),
                  pl.BlockSpec((1, d_inner), const),
                  pl.BlockSpec((LANE, d_inner), const)],
        out_specs=[pl.BlockSpec((1, 1, d_inner), lambda b: (b, 0, 0)),
                   pl.BlockSpec((1, ncs, conv_dim), lambda b: (b, 0, 0)),
                   pl.BlockSpec((1, hb, hp, n), lambda b: (b, 0, 0, 0))],
        out_shape=[jax.ShapeDtypeStruct((bsz, 1, d_inner), F32),
                   jax.ShapeDtypeStruct(conv_state.shape, F32),
                   jax.ShapeDtypeStruct(ssm_state.shape, F32)],
        compiler_params=_cp(("parallel",)),
        name="ssd_step",
    )(z3, xbc3, dt3, conv_state, ssm_state, conv_w, conv_b, dtb_pad, alog_lane, dskip_lane, norm_w, expand)


PROMPT_TM = 512
NORMMOD_TM = 1024
MATMUL_TN = 1024
ROUTE_TM = 256
EXPERT_NI = 8
EXPERT_SUB = 2
ATTN_T = 512
SAMPLE_PAD = LANE
WIN_SAMPLE_ROWS = 512
PAGES_PER_STEP = 8


def _pad_cols(w, n):
    return jnp.pad(w, ((0, 0), (0, n - w.shape[1])))


def _pad_rows(a, n):
    return jnp.pad(a, ((0, n - a.shape[0]),) + ((0, 0),) * (a.ndim - 1))


def _round_up(n, m):
    return -(-n // m) * m


def kernel(x_prompt, x_sample, c_prompt, c_sample, cache_win_k, cache_win_v, state_conv, state_ssm,
           cache_diff_k, cache_diff_v, page_table, norm1_w, norm2_w, w_ada, b_ada, w_in_even, w_out_even,
           conv_w, conv_b, dt_bias, a_log, d_skip, ssm_norm_w, w_in_odd, w_out_odd, lambda_q1, lambda_k1,
           lambda_q2, lambda_k2, subln_w, peer_wq, peer_keys, peer_u, peer_v, final_norm_w):
    bsz, seq, d = x_prompt.shape
    sb, st, _ = x_sample.shape
    assert st == 1, "one new token per sample sequence"
    depth = w_ada.shape[0]
    h_a = cache_win_k.shape[3]
    a_width = h_a * HD_A
    d_inner = ssm_norm_w.shape[1]
    conv_dim = conv_w.shape[2]
    h_b = dt_bias.shape[1]
    h_c = cache_diff_k.shape[3]
    c_width = h_c * 2 * HD_C
    wb_p = min(max(w for w, _ in DILATED_PATTERNS), seq)
    ncs = conv_w.shape[1] - 1

    xp = x_prompt.reshape(bsz * seq, d)
    xs = x_sample.reshape(sb, d)

    n_c = _round_up(bsz + sb, SUBLANE)
    c_all = _pad_rows(jnp.concatenate([c_prompt, c_sample], axis=0), n_c)
    mod = adaln_all(c_all, w_ada, b_ada)

    expand = (jnp.arange(LANE)[:, None] == (jnp.arange(d_inner) // SSM_HEADDIM)[None, :]).astype(F32)

    wq_all = _split_bf16(jnp.swapaxes(peer_wq, 1, 2))
    nhc = peer_keys.shape[1] * peer_keys.shape[2]
    keys_all = _split_bf16(peer_keys.reshape(depth, nhc, peer_keys.shape[3], peer_keys.shape[4]))
    u_all = peer_u.astype(BF16)
    vt_all = jnp.swapaxes(peer_v, 1, 2).astype(BF16)

    outs = {k: [] for k in ("pwk", "pwv", "pcv", "pss", "swk", "swv", "scv", "sss", "sdk", "sdv")}
    pdk_buf = jnp.zeros((depth // 2, bsz * seq, c_width), F32)
    pdv_buf = jnp.zeros((depth // 2, bsz * seq, c_width), F32)
    for l in range(depth):
        parts = [mod[l, :, i * d:(i + 1) * d] for i in range(6)]
        pm = [p[:bsz][:, None, :] for p in parts]
        sm = [p[bsz:bsz + sb][None] for p in parts]
        n1 = norm1_w[l][None]
        n2 = norm2_w[l][None]
        if l % 2 == 0:
            e = l // 2
            even_in = w_in_even.shape[2]
            n_pad = _round_up(even_in, 7 * LANE)
            w_in = _pad_cols(w_in_even[e], n_pad).astype(BF16)
            tn = n_pad // 7
            wo = w_out_even[e].astype(BF16)
            dtb_pad = _pad_cols(dt_bias[e][None], LANE)
            alog_pad = _pad_cols(a_log[e][None], LANE)
            alog_lane = jnp.repeat(a_log[e], SSM_HEADDIM)[None]
            dsk_lane = jnp.repeat(d_skip[e], SSM_HEADDIM)[None]
            cw, cb, nw = conv_w[e], conv_b[e][None], ssm_norm_w[e][None]
            col_z = 3 * a_width

            proj, proj_bf = normmod_matmul(xp, n1, pm[0], pm[1], w_in, tm=min(NORMMOD_TM, seq), tn=tn,
                                           rows_per_group=seq)
            ya = dilated_attention_prompt(proj_bf, batch=bsz, seq=seq, heads=h_a, t=ATTN_T)
            yb, st_pairs = ssd_prompt(proj, cw, cb, dtb_pad, alog_pad, dsk_lane, nw,
                                      batch=bsz, seq=seq, d_inner=d_inner, col0=col_z)
            xp = matmul_residual([ya, yb], [wo[:a_width], wo[a_width:]], xp, pm[2],
                                 tm=PROMPT_TM, tn=d, rows_per_group=seq)
            proj3 = proj.reshape(bsz, seq, n_pad)
            outs["pwk"].append(proj3[:, seq - wb_p:, a_width:2 * a_width].reshape(bsz, wb_p, h_a, HD_A))
            outs["pwv"].append(proj3[:, seq - wb_p:, 2 * a_width:3 * a_width].reshape(bsz, wb_p, h_a, HD_A))
            outs["pcv"].append(proj3[:, seq - ncs:, col_z + d_inner:col_z + d_inner + conv_dim])
            outs["pss"].append(_state_from_pairs(st_pairs))

            sproj, _ = normmod_matmul(xs, n1, sm[0], sm[1], w_in, tm=sb, tn=tn, rows_per_group=sb)
            q3 = sproj[:, :a_width].reshape(sb, h_a, HD_A)
            k3 = sproj[:, a_width:2 * a_width].reshape(sb, h_a, HD_A)
            v3 = sproj[:, 2 * a_width:3 * a_width].reshape(sb, h_a, HD_A)
            ya_s = dilated_attention_sample(q3, k3, v3, cache_win_k, cache_win_v, e,
                                            tr=min(WIN_SAMPLE_ROWS, cache_win_k.shape[2]))
            z3 = sproj[:, col_z:col_z + d_inner][:, None, :]
            xbc3 = sproj[:, col_z + d_inner:col_z + d_inner + conv_dim][:, None, :]
            dt3 = sproj[:, col_z + d_inner + conv_dim:col_z + d_inner + conv_dim + LANE][:, None, :]
            yb_s, cs_new, ss_new = ssd_step(z3, xbc3, dt3, state_conv[e], state_ssm[e], cw, cb, dtb_pad,
                                            alog_lane, dsk_lane, nw, expand)
            xs = matmul_residual([ya_s.reshape(sb, a_width).astype(BF16), yb_s.reshape(sb, d_inner).astype(BF16)],
                                 [wo[:a_width], wo[a_width:]], xs, sm[2], tm=sb, tn=MATMUL_TN, rows_per_group=sb)
            outs["swk"].append(k3[:, None])
            outs["swv"].append(v3[:, None])
            outs["scv"].append(cs_new)
            outs["sss"].append(ss_new)
        else:
            o_ = l // 2
            lam_init = 0.8 - 0.6 * math.exp(-0.3 * l)
            w_in = w_in_odd[o_].astype(BF16)
            wo = w_out_odd[o_].astype(BF16)
            lamp = jnp.stack([lambda_q1[o_], lambda_k1[o_], lambda_q2[o_], lambda_k2[o_]])
            sw = subln_w[o_][None]

            proj_bf, pdk_buf, pdv_buf = normmod_matmul_kv(xp, n1, pm[0], pm[1], w_in, pdk_buf, pdv_buf, o_,
                                                          tm=min(NORMMOD_TM, seq), tn=MATMUL_TN, rows_per_group=seq)
            o = diff_attention_prompt(proj_bf, lamp, sw, batch=bsz, seq=seq, heads=h_c, t=ATTN_T, lam_init=lam_init)
            xp = matmul_residual([o], [wo], xp, pm[2], tm=PROMPT_TM, tn=d, rows_per_group=seq)

            sproj, _ = normmod_matmul(xs, n1, sm[0], sm[1], w_in, tm=sb, tn=MATMUL_TN, rows_per_group=sb)
            q3 = sproj[:, :c_width].reshape(sb, h_c, 2 * HD_C)
            k3 = sproj[:, c_width:2 * c_width].reshape(sb, h_c, 2 * HD_C)
            v3 = sproj[:, 2 * c_width:3 * c_width].reshape(sb, h_c, 2 * HD_C)
            o_s = diff_attention_sample(page_table, q3, k3, v3, lamp, sw, cache_diff_k, cache_diff_v,
                                        layer=o_, pp=PAGES_PER_STEP, lam_init=lam_init)
            xs = matmul_residual([o_s.reshape(sb, c_width).astype(BF16)], [wo], xs, sm[2],
                                 tm=sb, tn=MATMUL_TN, rows_per_group=sb)
            outs["sdk"].append(k3[:, None])
            outs["sdv"].append(v3[:, None])

        routed = peer_route(xp, n2, pm[3], pm[4], wq_all, keys_all, l, tm=ROUTE_TM, rows_per_group=seq)
        xp = peer_experts(*routed, u_all, vt_all, l, xp, pm[5], tm=PROMPT_TM, ni=EXPERT_NI, sub=EXPERT_SUB,
                          rows_per_group=seq)

        xs_pad = _pad_rows(xs, SAMPLE_PAD)
        smp = [jnp.pad(m_, ((0, 0), (0, SAMPLE_PAD - sb), (0, 0))) for m_ in sm[3:6]]
        routed = peer_route(xs_pad, n2, smp[0], smp[1], wq_all, keys_all, l, tm=SAMPLE_PAD, rows_per_group=SAMPLE_PAD)
        xs = peer_experts(*routed, u_all, vt_all, l, xs_pad, smp[2], tm=SAMPLE_PAD, ni=EXPERT_NI, sub=EXPERT_SUB,
                          rows_per_group=SAMPLE_PAD)[:sb]

    y_prompt = rmsnorm_rows(xp, final_norm_w[None], tm=PROMPT_TM).reshape(bsz, seq, d)
    y_sample = rmsnorm_rows(xs, final_norm_w[None], tm=sb).reshape(sb, st, d)
    stk = lambda k: jnp.stack(outs[k])
    kv_shape = (depth // 2, bsz, seq, h_c, 2 * HD_C)
    return (y_prompt, y_sample, stk("pwk"), stk("pwv"), stk("pcv"), stk("pss"),
            pdk_buf.reshape(kv_shape), pdv_buf.reshape(kv_shape),
            stk("swk"), stk("swv"), stk("scv"), stk("sss"), stk("sdk"), stk("sdv"))
```
